```python
import jax, jax.numpy as jnp
from jax import lax
import numpy as np

D_MODEL = 1024
BATCH = 2
SEQ = 16384
DEPTH = 2
DEC_BATCH = 8
DEC_SEQ = 4096
PAST_LEN = 128

CHUNK = 128
A_HEADS = 8
A_HEAD_DIM = D_MODEL // A_HEADS
A_WIDTH = A_HEADS * A_HEAD_DIM
B_GROUPS = 4
B_WIDTH = D_MODEL
B_GROUP_DIM = B_WIDTH // B_GROUPS
N_BRANCH = 2
D_FF = ((8 * D_MODEL // 3 + 255) // 256) * 256
IN_WIDTH = 2 * A_WIDTH + B_WIDTH + N_BRANCH * D_MODEL
EPS = 1e-6

kernel_name = "hybrid_gmlp_fnet_macaron_encoder"


def rms_norm(x, g):
    x32 = x.astype(jnp.float32)
    y = x32 * lax.rsqrt(jnp.mean(x32 * x32, axis=-1, keepdims=True) + EPS)
    return (y * g.astype(jnp.float32)).astype(x.dtype)


def swiglu_ffn(h, w_gate, w_up, w_down):
    return (jax.nn.silu(h @ w_gate) * (h @ w_up)) @ w_down


def spatial_gating(u, v, g_v, w_s, b_s):
    bsz, s, _ = u.shape
    v = rms_norm(v, g_v).reshape(bsz, s // CHUNK, CHUNK, A_HEADS, A_HEAD_DIM)
    v = jnp.einsum('hpq,bcqhd->bcphd', w_s, v) + b_s.T[None, None, :, :, None]
    return u * v.reshape(bsz, s, A_WIDTH)


def fourier_mix(z):
    bsz, s, _ = z.shape
    zg = z.astype(jnp.float32).reshape(bsz, s, B_GROUPS, B_GROUP_DIM)
    f = jnp.fft.fft2(zg, axes=(1, 3), norm="ortho").real
    return f.reshape(bsz, s, B_WIDTH).astype(z.dtype)


def hybrid_mixer(h, w_in, g_v, w_s, b_s, w_branch_a, w_branch_b, w_out):
    z = h @ w_in
    splits = [A_WIDTH, 2 * A_WIDTH, 2 * A_WIDTH + B_WIDTH, 2 * A_WIDTH + B_WIDTH + D_MODEL]
    u, v, z_b, g_a, g_b = jnp.split(z, splits, axis=-1)
    y_a = spatial_gating(jax.nn.gelu(u, approximate=False), jax.nn.gelu(v, approximate=False),
                         g_v, w_s, b_s) @ w_branch_a
    y_b = fourier_mix(z_b) @ w_branch_b
    merged = jax.nn.sigmoid(g_a) * y_a + jax.nn.sigmoid(g_b) * y_b
    return merged @ w_out


def setup_inputs(seed: int = 0) -> dict:
    key = jax.random.key(seed)
    ks = jax.random.split(key, 24)

    def nrm(k, shape, scale):
        return jax.random.normal(k, shape, jnp.float32) * scale

    def gain(k, shape):
        return 1.0 + nrm(k, shape, 0.02)

    return {
        "x_prompt": nrm(ks[0], (BATCH, SEQ, D_MODEL), 1.0),
        "x_sample": nrm(ks[1], (DEC_BATCH, DEC_SEQ, D_MODEL), 1.0),
        "ffn1_norm": gain(ks[2], (DEPTH, D_MODEL)),
        "ffn1_w_gate": nrm(ks[3], (DEPTH, D_MODEL, D_FF), D_MODEL ** -0.5),
        "ffn1_w_up": nrm(ks[4], (DEPTH, D_MODEL, D_FF), D_MODEL ** -0.5),
        "ffn1_w_down": nrm(ks[5], (DEPTH, D_FF, D_MODEL), D_FF ** -0.5),
        "mix_norm": gain(ks[6], (DEPTH, D_MODEL)),
        "w_in": nrm(ks[7], (DEPTH, D_MODEL, IN_WIDTH), D_MODEL ** -0.5),
        "sgu_norm": gain(ks[8], (DEPTH, A_WIDTH)),
        "sgu_w": nrm(ks[9], (DEPTH, A_HEADS, CHUNK, CHUNK), CHUNK ** -0.5),
        "sgu_b": 1.0 + nrm(ks[10], (DEPTH, A_HEADS, CHUNK), 0.1),
        "w_branch_a": nrm(ks[11], (DEPTH, A_WIDTH, D_MODEL), A_WIDTH ** -0.5),
        "w_branch_b": nrm(ks[12], (DEPTH, B_WIDTH, D_MODEL), B_WIDTH ** -0.5),
        "w_out": nrm(ks[13], (DEPTH, D_MODEL, D_MODEL), D_MODEL ** -0.5),
        "ffn2_norm": gain(ks[14], (DEPTH, D_MODEL)),
        "ffn2_w_gate": nrm(ks[15], (DEPTH, D_MODEL, D_FF), D_MODEL ** -0.5),
        "ffn2_w_up": nrm(ks[16], (DEPTH, D_MODEL, D_FF), D_MODEL ** -0.5),
        "ffn2_w_down": nrm(ks[17], (DEPTH, D_FF, D_MODEL), D_FF ** -0.5),
        "final_norm": gain(ks[18], (D_MODEL,)),
    }


def reference(x_prompt, x_sample, ffn1_norm, ffn1_w_gate, ffn1_w_up, ffn1_w_down,
              mix_norm, w_in, sgu_norm, sgu_w, sgu_b, w_branch_a, w_branch_b, w_out,
              ffn2_norm, ffn2_w_gate, ffn2_w_up, ffn2_w_down, final_norm):
    def trunk(x):
        for l in range(DEPTH):
            x = x + 0.5 * swiglu_ffn(rms_norm(x, ffn1_norm[l]),
                                     ffn1_w_gate[l], ffn1_w_up[l], ffn1_w_down[l])
            x = x + hybrid_mixer(rms_norm(x, mix_norm[l]), w_in[l], sgu_norm[l],
                                 sgu_w[l], sgu_b[l], w_branch_a[l], w_branch_b[l], w_out[l])
            x = x + 0.5 * swiglu_ffn(rms_norm(x, ffn2_norm[l]),
                                     ffn2_w_gate[l], ffn2_w_up[l], ffn2_w_down[l])
        return rms_norm(x, final_norm)

    y_prompt = trunk(x_prompt)
    y_sample = trunk(x_sample)
    return (y_prompt, y_sample)
```

```python
import functools

import numpy as np
import jax
import jax.numpy as jnp
from jax import lax
from jax.experimental import pallas as pl
from jax.experimental.pallas import tpu as pltpu

D_MODEL = 1024
CHUNK = 128
A_HEADS = 8
A_HEAD_DIM = D_MODEL // A_HEADS
B_GROUPS = 4
B_GROUP_DIM = D_MODEL // B_GROUPS
D_FF = 2816
EPS = 1e-6

TOKEN_TILE = 512
FF_COLS = 256
FFT_N1 = 128
FFT_SLABS = 8
VMEM_LIMIT_BYTES = 56 * 1024 * 1024

_BF16 = jnp.bfloat16
_F32 = jnp.float32


def _dot(a, b):
    return jnp.dot(a, b, preferred_element_type=_F32)


def _rms(x, g):
    return x * lax.rsqrt(jnp.mean(x * x, axis=-1, keepdims=True) + EPS) * g


def _gelu(x):
    return 0.5 * x * (1.0 + lax.erf(x * np.float32(np.sqrt(0.5))))


def _const_spec(shape):
    return pl.BlockSpec(shape, lambda *_: (0,) * len(shape))


def _params():
    return pltpu.CompilerParams(dimension_semantics=("arbitrary",),
                                vmem_limit_bytes=VMEM_LIMIT_BYTES)


def _ffn_kernel(x_ref, g_ref, wg_ref, wu_ref, wd_ref, fg_ref, o_ref, a_ref, *, final):
    x = x_ref[...]
    h = _rms(x, g_ref[...]).astype(_BF16)
    for c in range(0, D_FF, FF_COLS):
        gate = _dot(h, wg_ref[:, c:c + FF_COLS])
        up = _dot(h, wu_ref[:, c:c + FF_COLS])
        a_ref[:, c:c + FF_COLS] = (jax.nn.silu(gate) * up).astype(_BF16)
    out = x + 0.5 * _dot(a_ref[...], wd_ref[...])
    if final:
        out = _rms(out, fg_ref[...])
    o_ref[...] = out


def _ffn(x, g, wg, wu, wd, fg, final):
    t = x.shape[0]
    tile = pl.BlockSpec((TOKEN_TILE, D_MODEL), lambda i: (i, 0))
    return pl.pallas_call(
        functools.partial(_ffn_kernel, final=final),
        grid=(t // TOKEN_TILE,),
        in_specs=[tile, _const_spec((1, D_MODEL)), _const_spec((D_MODEL, D_FF)),
                  _const_spec((D_MODEL, D_FF)), _const_spec((D_FF, D_MODEL)),
                  _const_spec((1, D_MODEL))],
        out_specs=tile,
        out_shape=jax.ShapeDtypeStruct((t, D_MODEL), _F32),
        scratch_shapes=[pltpu.VMEM((TOKEN_TILE, D_FF), _BF16)],
        compiler_params=_params(),
        name="ffn",
    )(x, g, wg, wu, wd, fg)


def _mix_in_kernel(x_ref, g_ref, win_ref, gv_ref, ws_ref, bs_ref, wa_ref, dft_ref,
                   gaya_ref, gb_ref, xr_ref, xi_ref, u_scr, v_scr, sg_scr):
    d = D_MODEL
    h = _rms(x_ref[...], g_ref[...]).astype(_BF16)
    u_scr[...] = _gelu(_dot(h, win_ref[:, 0:d]))
    v = _gelu(_dot(h, win_ref[:, d:2 * d]))
    v_scr[...] = _rms(v, gv_ref[...]).astype(_BF16)

    n_chunks = TOKEN_TILE // CHUNK
    for hd in range(A_HEADS):
        lanes = slice(hd * A_HEAD_DIM, (hd + 1) * A_HEAD_DIM)
        rhs = jnp.concatenate(
            [v_scr[c * CHUNK:(c + 1) * CHUNK, lanes] for c in range(n_chunks)], axis=1)
        mixed = _dot(ws_ref[hd], rhs)
        for c in range(n_chunks):
            rows = slice(c * CHUNK, (c + 1) * CHUNK)
            sp = mixed[:, c * A_HEAD_DIM:(c + 1) * A_HEAD_DIM] + bs_ref[:, lanes]
            sg_scr[rows, lanes] = (u_scr[rows, lanes] * sp).astype(_BF16)

    y_a = _dot(sg_scr[...], wa_ref[...])
    gaya_ref[...] = jax.nn.sigmoid(_dot(h, win_ref[:, 3 * d:4 * d])) * y_a
    gb_ref[...] = jax.nn.sigmoid(_dot(h, win_ref[:, 4 * d:5 * d]))

    z_b = _dot(h, win_ref[:, 2 * d:3 * d]).astype(_BF16)
    gd = B_GROUP_DIM
    dft = dft_ref[...].astype(_BF16)
    for grp in range(B_GROUPS):
        cols = slice(grp * gd, (grp + 1) * gd)
        spec = _dot(z_b[:, cols], dft)
        xr_ref[:, cols] = spec[:, :gd].astype(_BF16)
        xi_ref[:, cols] = spec[:, gd:].astype(_BF16)


def _mix_in(x, g, win, gv, ws, bs, wa, dft):
    t = x.shape[0]
    d = D_MODEL
    tile = pl.BlockSpec((TOKEN_TILE, d), lambda i: (i, 0))
    return pl.pallas_call(
        _mix_in_kernel,
        grid=(t // TOKEN_TILE,),
        in_specs=[tile, _const_spec((1, d)), _const_spec((d, 5 * d)), _const_spec((1, d)),
                  _const_spec((A_HEADS, CHUNK, CHUNK)), _const_spec((CHUNK, d)),
                  _const_spec((d, d)), _const_spec((B_GROUP_DIM, 2 * B_GROUP_DIM))],
        out_specs=[tile, tile, tile, tile],
        out_shape=[jax.ShapeDtypeStruct((t, d), _F32), jax.ShapeDtypeStruct((t, d), _F32),
                   jax.ShapeDtypeStruct((t, d), _BF16), jax.ShapeDtypeStruct((t, d), _BF16)],
        scratch_shapes=[pltpu.VMEM((TOKEN_TILE, d), _F32), pltpu.VMEM((TOKEN_TILE, d), _BF16),
                        pltpu.VMEM((TOKEN_TILE, d), _BF16)],
        compiler_params=_params(),
        name="mix_in",
    )(x, g, win, gv, ws, bs, wa, dft)


def _fft_a_kernel(xr_ref, xi_ref, f_ref, tc_ref, ts_ref, br_ref, bi_ref):
    d = D_MODEL
    n1 = FFT_N1
    fmat = f_ref[...].astype(_BF16)
    for t in range(FFT_SLABS):
        cols = slice(t * d, (t + 1) * d)
        x = jnp.concatenate([xr_ref[:, cols], xi_ref[:, cols]], axis=0)
        a = _dot(fmat, x)
        ar, ai = a[:n1], a[n1:]
        tc = tc_ref[:, t:t + 1]
        ts = ts_ref[:, t:t + 1]
        br_ref[:, cols] = (ar * tc + ai * ts).astype(_BF16)
        bi_ref[:, cols] = (ai * tc - ar * ts).astype(_BF16)


def _fft_a(xr, xi, fmat, tw_c, tw_s, batch, seq):
    d = D_MODEL
    n1, n2 = FFT_N1, seq // FFT_N1
    m = FFT_SLABS
    view = (batch, n1, n2 * d)
    blk = pl.BlockSpec((None, n1, m * d), lambda b, j: (b, 0, j))
    tw = pl.BlockSpec((None, n1, m), lambda b, j: (j, 0, 0))
    br, bi = pl.pallas_call(
        _fft_a_kernel,
        grid=(batch, n2 // m),
        in_specs=[blk, blk, pl.BlockSpec((2 * n1, 2 * n1), lambda b, j: (0, 0)), tw, tw],
        out_specs=[blk, blk],
        out_shape=[jax.ShapeDtypeStruct(view, _BF16), jax.ShapeDtypeStruct(view, _BF16)],
        compiler_params=pltpu.CompilerParams(dimension_semantics=("arbitrary", "arbitrary"),
                                             vmem_limit_bytes=VMEM_LIMIT_BYTES),
        name="fft_a",
    )(xr.reshape(view), xi.reshape(view), fmat, tw_c, tw_s)
    return br.reshape(batch * seq, d), bi.reshape(batch * seq, d)


def _fft_b_kernel(br_ref, bi_ref, g_ref, f_ref, *, n2):
    d = D_MODEL
    per_slab = CHUNK // n2
    gmat = g_ref[...].astype(_BF16)
    for t in range(FFT_SLABS):
        rows = slice(t * CHUNK, (t + 1) * CHUNK)
        b = jnp.concatenate([br_ref[rows, :], bi_ref[rows, :]], axis=0)
        y = _dot(gmat, b).astype(_BF16)
        for a in range(per_slab):
            k1 = t * per_slab + a
            f_ref[:, k1 * d:(k1 + 1) * d] = y[a * n2:(a + 1) * n2, :]


def _fft_b(br, bi, gmat, batch, seq):
    d = D_MODEL
    n1, n2 = FFT_N1, seq // FFT_N1
    m = FFT_SLABS
    k1_per_step = m * (CHUNK // n2)
    in_view = (batch, seq, d)
    out_view = (batch, n2, n1 * d)
    f = pl.pallas_call(
        functools.partial(_fft_b_kernel, n2=n2),
        grid=(batch, n1 // k1_per_step),
        in_specs=[pl.BlockSpec((None, m * CHUNK, d), lambda b, j: (b, j, 0)),
                  pl.BlockSpec((None, m * CHUNK, d), lambda b, j: (b, j, 0)),
                  pl.BlockSpec((CHUNK, 2 * CHUNK), lambda b, j: (0, 0))],
        out_specs=pl.BlockSpec((None, n2, k1_per_step * d), lambda b, j: (b, 0, j)),
        out_shape=jax.ShapeDtypeStruct(out_view, _BF16),
        compiler_params=pltpu.CompilerParams(dimension_semantics=("arbitrary", "arbitrary"),
                                             vmem_limit_bytes=VMEM_LIMIT_BYTES),
        name="fft_b",
    )(br.reshape(in_view), bi.reshape(in_view), gmat)
    return f.reshape(batch * seq, d)


def _mix_out_kernel(x_ref, gaya_ref, gb_ref, f_ref, wb_ref, wo_ref, o_ref):
    y_b = _dot(f_ref[...], wb_ref[...])
    merged = (gaya_ref[...] + gb_ref[...] * y_b).astype(_BF16)
    o_ref[...] = x_ref[...] + _dot(merged, wo_ref[...])


def _mix_out(x, gaya, gb, f, wb, wo):
    t = x.shape[0]
    d = D_MODEL
    tile = pl.BlockSpec((TOKEN_TILE, d), lambda i: (i, 0))
    return pl.pallas_call(
        _mix_out_kernel,
        grid=(t // TOKEN_TILE,),
        in_specs=[tile, tile, tile, tile, _const_spec((d, d)), _const_spec((d, d))],
        out_specs=tile,
        out_shape=jax.ShapeDtypeStruct((t, d), _F32),
        compiler_params=_params(),
        name="mix_out",
    )(x, gaya, gb, f, wb, wo)


def _cos_sin(num, den):
    ang = 2.0 * np.pi * (np.asarray(num, dtype=np.int64) % den).astype(np.float64) / den
    return np.cos(ang), np.sin(ang)


def _channel_dft():
    n = B_GROUP_DIM
    c, s = _cos_sin(np.outer(np.arange(n), np.arange(n)), n)
    return jnp.asarray(np.concatenate([c, -s], axis=1) / np.sqrt(n), dtype=_F32)


def _position_dft(seq):
    n1, n2 = FFT_N1, seq // FFT_N1
    c1, s1 = _cos_sin(np.outer(np.arange(n1), np.arange(n1)), n1)
    fmat = np.block([[c1, s1], [-s1, c1]]) / np.sqrt(seq)
    tc, ts = _cos_sin(np.outer(np.arange(n1), np.arange(n2)), seq)
    m = FFT_SLABS
    tc = tc.reshape(n1, n2 // m, m).transpose(1, 0, 2)
    ts = ts.reshape(n1, n2 // m, m).transpose(1, 0, 2)
    c2, s2 = _cos_sin(np.outer(np.arange(n2), np.arange(n2)), n2)
    eye = np.eye(CHUNK // n2)
    gmat = np.concatenate([np.kron(eye, c2), np.kron(eye, s2)], axis=1)
    return tuple(jnp.asarray(t, dtype=_F32) for t in (fmat, tc, ts, gmat))


def kernel(x_prompt, x_sample, ffn1_norm, ffn1_w_gate, ffn1_w_up, ffn1_w_down, mix_norm, w_in,
           sgu_norm, sgu_w, sgu_b, w_branch_a, w_branch_b, w_out, ffn2_norm, ffn2_w_gate,
           ffn2_w_up, ffn2_w_down, final_norm):
    depth = w_in.shape[0]
    d = D_MODEL
    bf = lambda w: w.astype(_BF16)
    row = lambda v: v.reshape(1, d).astype(_F32)
    dft = _channel_dft()
    bias = jnp.repeat(jnp.swapaxes(sgu_b, 1, 2), A_HEAD_DIM, axis=2).astype(_F32)
    weights = dict(
        wg1=bf(ffn1_w_gate), wu1=bf(ffn1_w_up), wd1=bf(ffn1_w_down), win=bf(w_in), ws=bf(sgu_w),
        wa=bf(w_branch_a), wb=bf(w_branch_b), wo=bf(w_out), wg2=bf(ffn2_w_gate),
        wu2=bf(ffn2_w_up), wd2=bf(ffn2_w_down))
    fg = row(final_norm)

    def trunk(x3):
        batch, seq, _ = x3.shape
        fmat, tw_c, tw_s, gmat = _position_dft(seq)
        x = x3.reshape(batch * seq, d)
        for l in range(depth):
            w = {k: v[l] for k, v in weights.items()}
            x = _ffn(x, row(ffn1_norm[l]), w["wg1"], w["wu1"], w["wd1"], fg, final=False)
            gaya, gb, xr, xi = _mix_in(x, row(mix_norm[l]), w["win"], row(sgu_norm[l]), w["ws"],
                                       bias[l], w["wa"], dft)
            br, bi = _fft_a(xr, xi, fmat, tw_c, tw_s, batch, seq)
            f = _fft_b(br, bi, gmat, batch, seq)
            x = _mix_out(x, gaya, gb, f, w["wb"], w["wo"])
            x = _ffn(x, row(ffn2_norm[l]), w["wg2"], w["wu2"], w["wd2"], fg,
                     final=(l == depth - 1))
        return x.reshape(batch, seq, d)

    return (trunk(x_prompt), trunk(x_sample))
```

```python
import functools

import numpy as np
import jax
import jax.numpy as jnp
from jax import lax
from jax.experimental import pallas as pl
from jax.experimental.pallas import tpu as pltpu

D_MODEL = 1024
CHUNK = 128
A_HEADS = 8
A_HEAD_DIM = D_MODEL // A_HEADS
B_GROUPS = 4
B_GROUP_DIM = D_MODEL // B_GROUPS
D_FF = 2816
EPS = 1e-6

TOKEN_TILE = 512
FF_COLS = 256
FFT_N1 = 128
FFT_S2_TILE = 16
FFT_B_ROWS = 2048
FFT_COLS = 512
VMEM_LIMIT_BYTES = 56 * 1024 * 1024

_BF16 = jnp.bfloat16
_F32 = jnp.float32


def _dot(a, b):
    return jnp.dot(a, b, preferred_element_type=_F32)


def _rms(x, g):
    return x * lax.rsqrt(jnp.mean(x * x, axis=-1, keepdims=True) + EPS) * g


def _gelu(x):
    return 0.5 * x * (1.0 + lax.erf(x * np.float32(np.sqrt(0.5))))


def _const_spec(shape):
    return pl.BlockSpec(shape, lambda *_: (0,) * len(shape))


def _params():
    return pltpu.CompilerParams(dimension_semantics=("arbitrary",),
                                vmem_limit_bytes=VMEM_LIMIT_BYTES)


def _ffn_kernel(x_ref, g_ref, wg_ref, wu_ref, wd_ref, fg_ref, o_ref, a_ref, *, final):
    x = x_ref[...]
    h = _rms(x, g_ref[...]).astype(_BF16)
    for c in range(0, D_FF, FF_COLS):
        gate = _dot(h, wg_ref[:, c:c + FF_COLS])
        up = _dot(h, wu_ref[:, c:c + FF_COLS])
        a_ref[:, c:c + FF_COLS] = (jax.nn.silu(gate) * up).astype(_BF16)
    out = x + 0.5 * _dot(a_ref[...], wd_ref[...])
    if final:
        out = _rms(out, fg_ref[...])
    o_ref[...] = out


def _ffn(x, g, wg, wu, wd, fg, final):
    t = x.shape[0]
    tile = pl.BlockSpec((TOKEN_TILE, D_MODEL), lambda i: (i, 0))
    return pl.pallas_call(
        functools.partial(_ffn_kernel, final=final),
        grid=(t // TOKEN_TILE,),
        in_specs=[tile, _const_spec((1, D_MODEL)), _const_spec((D_MODEL, D_FF)),
                  _const_spec((D_MODEL, D_FF)), _const_spec((D_FF, D_MODEL)),
                  _const_spec((1, D_MODEL))],
        out_specs=tile,
        out_shape=jax.ShapeDtypeStruct((t, D_MODEL), _F32),
        scratch_shapes=[pltpu.VMEM((TOKEN_TILE, D_FF), _BF16)],
        compiler_params=_params(),
        name="ffn",
    )(x, g, wg, wu, wd, fg)


def _mix_in_kernel(x_ref, g_ref, win_ref, gv_ref, ws_ref, bs_ref, wa_ref, dft_ref,
                   gaya_ref, gb_ref, xr_ref, xi_ref, u_scr, v_scr, sg_scr):
    d = D_MODEL
    h = _rms(x_ref[...], g_ref[...]).astype(_BF16)
    u_scr[...] = _gelu(_dot(h, win_ref[:, 0:d]))
    v = _gelu(_dot(h, win_ref[:, d:2 * d]))
    v_scr[...] = _rms(v, gv_ref[...]).astype(_BF16)

    n_chunks = TOKEN_TILE // CHUNK
    for hd in range(A_HEADS):
        lanes = slice(hd * A_HEAD_DIM, (hd + 1) * A_HEAD_DIM)
        rhs = jnp.concatenate(
            [v_scr[c * CHUNK:(c + 1) * CHUNK, lanes] for c in range(n_chunks)], axis=1)
        mixed = _dot(ws_ref[hd], rhs)
        for c in range(n_chunks):
            rows = slice(c * CHUNK, (c + 1) * CHUNK)
            sp = mixed[:, c * A_HEAD_DIM:(c + 1) * A_HEAD_DIM] + bs_ref[:, lanes]
            sg_scr[rows, lanes] = (u_scr[rows, lanes] * sp).astype(_BF16)

    y_a = _dot(sg_scr[...], wa_ref[...])
    gaya_ref[...] = jax.nn.sigmoid(_dot(h, win_ref[:, 3 * d:4 * d])) * y_a
    gb_ref[...] = jax.nn.sigmoid(_dot(h, win_ref[:, 4 * d:5 * d]))

    z_b = _dot(h, win_ref[:, 2 * d:3 * d]).astype(_BF16)
    gd = B_GROUP_DIM
    dft = dft_ref[...].astype(_BF16)
    for grp in range(B_GROUPS):
        cols = slice(grp * gd, (grp + 1) * gd)
        spec = _dot(z_b[:, cols], dft)
        xr_ref[:, cols] = spec[:, :gd].astype(_BF16)
        xi_ref[:, cols] = spec[:, gd:].astype(_BF16)


def _mix_in(x, g, win, gv, ws, bs, wa, dft):
    t = x.shape[0]
    d = D_MODEL
    tile = pl.BlockSpec((TOKEN_TILE, d), lambda i: (i, 0))
    return pl.pallas_call(
        _mix_in_kernel,
        grid=(t // TOKEN_TILE,),
        in_specs=[tile, _const_spec((1, d)), _const_spec((d, 5 * d)), _const_spec((1, d)),
                  _const_spec((A_HEADS, CHUNK, CHUNK)), _const_spec((CHUNK, d)),
                  _const_spec((d, d)), _const_spec((B_GROUP_DIM, 2 * B_GROUP_DIM))],
        out_specs=[tile, tile, tile, tile],
        out_shape=[jax.ShapeDtypeStruct((t, d), _F32), jax.ShapeDtypeStruct((t, d), _F32),
                   jax.ShapeDtypeStruct((t, d), _BF16), jax.ShapeDtypeStruct((t, d), _BF16)],
        scratch_shapes=[pltpu.VMEM((TOKEN_TILE, d), _F32), pltpu.VMEM((TOKEN_TILE, d), _BF16),
                        pltpu.VMEM((TOKEN_TILE, d), _BF16)],
        compiler_params=_params(),
        name="mix_in",
    )(x, g, win, gv, ws, bs, wa, dft)


def _fft_a_kernel(xr_ref, xi_ref, f_ref, tc_ref, ts_ref, br_ref, bi_ref):
    n1 = FFT_N1
    fmat = f_ref[...].astype(_BF16)
    xr = jnp.swapaxes(xr_ref[...], 0, 1)
    xi = jnp.swapaxes(xi_ref[...], 0, 1)
    br, bi = [], []
    for t in range(FFT_S2_TILE):
        a = _dot(fmat, jnp.concatenate([xr[t], xi[t]], axis=0))
        ar, ai = a[:n1], a[n1:]
        tc = tc_ref[:, t:t + 1]
        ts = ts_ref[:, t:t + 1]
        br.append((ar * tc + ai * ts).astype(_BF16))
        bi.append((ai * tc - ar * ts).astype(_BF16))
    br_ref[...] = jnp.swapaxes(jnp.stack(br), 0, 1)
    bi_ref[...] = jnp.swapaxes(jnp.stack(bi), 0, 1)


def _fft_a(xr, xi, fmat, tw_c, tw_s, batch, seq):
    d = D_MODEL
    n1, n2 = FFT_N1, seq // FFT_N1
    m = FFT_S2_TILE
    view = (batch, n1, n2, d)
    blk = pl.BlockSpec((None, n1, m, FFT_COLS), lambda b, j, c: (b, 0, j, c))
    tw = pl.BlockSpec((None, n1, m), lambda b, j, c: (j, 0, 0))
    br, bi = pl.pallas_call(
        _fft_a_kernel,
        grid=(batch, n2 // m, d // FFT_COLS),
        in_specs=[blk, blk, pl.BlockSpec((2 * n1, 2 * n1), lambda b, j, c: (0, 0)), tw, tw],
        out_specs=[blk, blk],
        out_shape=[jax.ShapeDtypeStruct(view, _BF16), jax.ShapeDtypeStruct(view, _BF16)],
        compiler_params=pltpu.CompilerParams(dimension_semantics=("arbitrary",) * 3,
                                             vmem_limit_bytes=VMEM_LIMIT_BYTES),
        name="fft_a",
    )(xr.reshape(view), xi.reshape(view), fmat, tw_c, tw_s)
    return br, bi


def _fft_b_kernel(br_ref, bi_ref, g_ref, f_ref, *, n2):
    gmat = g_ref[...].astype(_BF16)
    n_slabs = br_ref.shape[0] // CHUNK
    ys = []
    for t in range(n_slabs):
        rows = slice(t * CHUNK, (t + 1) * CHUNK)
        b = jnp.concatenate([br_ref[rows, :], bi_ref[rows, :]], axis=0)
        y = _dot(gmat, b).astype(_BF16)
        ys.append(y.reshape(CHUNK // n2, n2, y.shape[-1]))
    f_ref[...] = jnp.swapaxes(jnp.concatenate(ys, axis=0), 0, 1)


def _fft_b(br, bi, gmat, batch, seq):
    d = D_MODEL
    n1, n2 = FFT_N1, seq // FFT_N1
    k1s = FFT_B_ROWS // n2
    in_view = (batch, seq, d)
    out_view = (batch, n2, n1, d)
    in_blk = pl.BlockSpec((None, FFT_B_ROWS, FFT_COLS), lambda b, j, c: (b, j, c))
    f = pl.pallas_call(
        functools.partial(_fft_b_kernel, n2=n2),
        grid=(batch, n1 // k1s, d // FFT_COLS),
        in_specs=[in_blk, in_blk, pl.BlockSpec((CHUNK, 2 * CHUNK), lambda b, j, c: (0, 0))],
        out_specs=pl.BlockSpec((None, n2, k1s, FFT_COLS), lambda b, j, c: (b, 0, j, c)),
        out_shape=jax.ShapeDtypeStruct(out_view, _BF16),
        compiler_params=pltpu.CompilerParams(dimension_semantics=("arbitrary",) * 3,
                                             vmem_limit_bytes=VMEM_LIMIT_BYTES),
        name="fft_b",
    )(br.reshape(in_view), bi.reshape(in_view), gmat)
    return f.reshape(batch * seq, d)


def _mix_out_kernel(x_ref, gaya_ref, gb_ref, f_ref, wb_ref, wo_ref, o_ref):
    y_b = _dot(f_ref[...], wb_ref[...])
    merged = (gaya_ref[...] + gb_ref[...] * y_b).astype(_BF16)
    o_ref[...] = x_ref[...] + _dot(merged, wo_ref[...])


def _mix_out(x, gaya, gb, f, wb, wo):
    t = x.shape[0]
    d = D_MODEL
    tile = pl.BlockSpec((TOKEN_TILE, d), lambda i: (i, 0))
    return pl.pallas_call(
        _mix_out_kernel,
        grid=(t // TOKEN_TILE,),
        in_specs=[tile, tile, tile, tile, _const_spec((d, d)), _const_spec((d, d))],
        out_specs=tile,
        out_shape=jax.ShapeDtypeStruct((t, d), _F32),
        compiler_params=_params(),
        name="mix_out",
    )(x, gaya, gb, f, wb, wo)


def _cos_sin(num, den):
    ang = 2.0 * np.pi * (np.asarray(num, dtype=np.int64) % den).astype(np.float64) / den
    return np.cos(ang), np.sin(ang)


def _channel_dft():
    n = B_GROUP_DIM
    c, s = _cos_sin(np.outer(np.arange(n), np.arange(n)), n)
    return jnp.asarray(np.concatenate([c, -s], axis=1) / np.sqrt(n), dtype=_F32)


def _position_dft(seq):
    n1, n2 = FFT_N1, seq // FFT_N1
    c1, s1 = _cos_sin(np.outer(np.arange(n1), np.arange(n1)), n1)
    fmat = np.block([[c1, s1], [-s1, c1]]) / np.sqrt(seq)
    tc, ts = _cos_sin(np.outer(np.arange(n1), np.arange(n2)), seq)
    m = FFT_S2_TILE
    tc = tc.reshape(n1, n2 // m, m).transpose(1, 0, 2)
    ts = ts.reshape(n1, n2 // m, m).transpose(1, 0, 2)
    c2, s2 = _cos_sin(np.outer(np.arange(n2), np.arange(n2)), n2)
    eye = np.eye(CHUNK // n2)
    gmat = np.concatenate([np.kron(eye, c2), np.kron(eye, s2)], axis=1)
    return tuple(jnp.asarray(t, dtype=_F32) for t in (fmat, tc, ts, gmat))


def kernel(x_prompt, x_sample, ffn1_norm, ffn1_w_gate, ffn1_w_up, ffn1_w_down, mix_norm, w_in,
           sgu_norm, sgu_w, sgu_b, w_branch_a, w_branch_b, w_out, ffn2_norm, ffn2_w_gate,
           ffn2_w_up, ffn2_w_down, final_norm):
    depth = w_in.shape[0]
    d = D_MODEL
    bf = lambda w: w.astype(_BF16)
    row = lambda v: v.reshape(1, d).astype(_F32)
    dft = _channel_dft()
    bias = jnp.repeat(jnp.swapaxes(sgu_b, 1, 2), A_HEAD_DIM, axis=2).astype(_F32)
    weights = dict(
        wg1=bf(ffn1_w_gate), wu1=bf(ffn1_w_up), wd1=bf(ffn1_w_down), win=bf(w_in), ws=bf(sgu_w),
        wa=bf(w_branch_a), wb=bf(w_branch_b), wo=bf(w_out), wg2=bf(ffn2_w_gate),
        wu2=bf(ffn2_w_up), wd2=bf(ffn2_w_down))
    fg = row(final_norm)

    def trunk(x3):
        batch, seq, _ = x3.shape
        fmat, tw_c, tw_s, gmat = _position_dft(seq)
        x = x3.reshape(batch * seq, d)
        for l in range(depth):
            w = {k: v[l] for k, v in weights.items()}
            x = _ffn(x, row(ffn1_norm[l]), w["wg1"], w["wu1"], w["wd1"], fg, final=False)
            gaya, gb, xr, xi = _mix_in(x, row(mix_norm[l]), w["win"], row(sgu_norm[l]), w["ws"],
                                       bias[l], w["wa"], dft)
            br, bi = _fft_a(xr, xi, fmat, tw_c, tw_s, batch, seq)
            f = _fft_b(br, bi, gmat, batch, seq)
            x = _mix_out(x, gaya, gb, f, w["wb"], w["wo"])
            x = _ffn(x, row(ffn2_norm[l]), w["wg2"], w["wu2"], w["wd2"], fg,
                     final=(l == depth - 1))
        return x.reshape(batch, seq, d)

    return (trunk(x_prompt), trunk(x_sample))
```

```python
import functools

import numpy as np
import jax
import jax.numpy as jnp
from jax import lax
from jax.experimental import pallas as pl
from jax.experimental.pallas import tpu as pltpu

D_MODEL = 1024
CHUNK = 128
A_HEADS = 8
A_HEAD_DIM = D_MODEL // A_HEADS
B_GROUPS = 4
B_GROUP_DIM = D_MODEL // B_GROUPS
D_FF = 2816
EPS = 1e-6

FFN_TILE = 1024
MIX_TILE = 512
FF_COLS = 256
FFT_N1 = 128
FFT_S2_TILE = 16
FFT_B_ROWS = 2048
VMEM_LIMIT_BYTES = 56 * 1024 * 1024

_BF16 = jnp.bfloat16
_F32 = jnp.float32


def _dot(a, b):
    return jnp.dot(a, b, preferred_element_type=_F32)


def _rms(x, g):
    return x * lax.rsqrt(jnp.mean(x * x, axis=-1, keepdims=True) + EPS) * g


def _gelu(x):
    return 0.5 * x * (1.0 + lax.erf(x * np.float32(np.sqrt(0.5))))


def _resident(shape):
    return pl.BlockSpec(shape, lambda *_: (0,) * len(shape), pipeline_mode=pl.Buffered(1))


def _params(n_grid_dims=1):
    return pltpu.CompilerParams(dimension_semantics=("arbitrary",) * n_grid_dims,
                                vmem_limit_bytes=VMEM_LIMIT_BYTES)


def _swiglu_residual(x, g_ref, wg_ref, wu_ref, wd_ref, a_ref):
    h = _rms(x, g_ref[...]).astype(_BF16)
    for c in range(0, D_FF, FF_COLS):
        gate = _dot(h, wg_ref[:, c:c + FF_COLS])
        up = _dot(h, wu_ref[:, c:c + FF_COLS])
        a_ref[:, c:c + FF_COLS] = (jax.nn.silu(gate) * up).astype(_BF16)
    return x + 0.5 * _dot(a_ref[...], wd_ref[...])


def _ffn_kernel(x_ref, g_ref, wg_ref, wu_ref, wd_ref, o_ref, a_ref):
    o_ref[...] = _swiglu_residual(x_ref[...], g_ref, wg_ref, wu_ref, wd_ref, a_ref)


def _ffn_specs():
    return [_resident((1, D_MODEL)), _resident((D_MODEL, D_FF)), _resident((D_MODEL, D_FF)),
            _resident((D_FF, D_MODEL))]


def _ffn(x, g, wg, wu, wd):
    t = x.shape[0]
    tile = pl.BlockSpec((FFN_TILE, D_MODEL), lambda i: (i, 0))
    return pl.pallas_call(
        _ffn_kernel,
        grid=(t // FFN_TILE,),
        in_specs=[tile] + _ffn_specs(),
        out_specs=tile,
        out_shape=jax.ShapeDtypeStruct((t, D_MODEL), _F32),
        scratch_shapes=[pltpu.VMEM((FFN_TILE, D_FF), _BF16)],
        compiler_params=_params(),
        name="ffn",
    )(x, g, wg, wu, wd)


def _mix_in_kernel(x_ref, g_ref, win_ref, gv_ref, ws_ref, bs_ref, wa_ref, dft_ref,
                   gaya_ref, gb_ref, xr_ref, xi_ref, u_scr, v_scr, sg_scr):
    d = D_MODEL
    h = _rms(x_ref[...], g_ref[...]).astype(_BF16)
    u_scr[...] = _gelu(_dot(h, win_ref[:, 0:d]))
    v = _gelu(_dot(h, win_ref[:, d:2 * d]))
    v_scr[...] = _rms(v, gv_ref[...]).astype(_BF16)

    n_chunks = MIX_TILE // CHUNK
    for hd in range(A_HEADS):
        lanes = slice(hd * A_HEAD_DIM, (hd + 1) * A_HEAD_DIM)
        rhs = jnp.concatenate(
            [v_scr[c * CHUNK:(c + 1) * CHUNK, lanes] for c in range(n_chunks)], axis=1)
        mixed = _dot(ws_ref[hd], rhs)
        for c in range(n_chunks):
            rows = slice(c * CHUNK, (c + 1) * CHUNK)
            sp = mixed[:, c * A_HEAD_DIM:(c + 1) * A_HEAD_DIM] + bs_ref[:, lanes]
            sg_scr[rows, lanes] = (u_scr[rows, lanes] * sp).astype(_BF16)

    y_a = _dot(sg_scr[...], wa_ref[...])
    gaya_ref[...] = (jax.nn.sigmoid(_dot(h, win_ref[:, 3 * d:4 * d])) * y_a).astype(_BF16)
    gb_ref[...] = jax.nn.sigmoid(_dot(h, win_ref[:, 4 * d:5 * d])).astype(_BF16)

    z_b = _dot(h, win_ref[:, 2 * d:3 * d]).astype(_BF16)
    gd = B_GROUP_DIM
    dft = dft_ref[...].astype(_BF16)
    for grp in range(B_GROUPS):
        cols = slice(grp * gd, (grp + 1) * gd)
        spec = _dot(z_b[:, cols], dft)
        xr_ref[:, cols] = spec[:, :gd].astype(_BF16)
        xi_ref[:, cols] = spec[:, gd:].astype(_BF16)


def _mix_in(x, g, win, gv, ws, bs, wa, dft):
    t = x.shape[0]
    d = D_MODEL
    tile = pl.BlockSpec((MIX_TILE, d), lambda i: (i, 0))
    return pl.pallas_call(
        _mix_in_kernel,
        grid=(t // MIX_TILE,),
        in_specs=[tile, _resident((1, d)), _resident((d, 5 * d)), _resident((1, d)),
                  _resident((A_HEADS, CHUNK, CHUNK)), _resident((CHUNK, d)),
                  _resident((d, d)), _resident((B_GROUP_DIM, 2 * B_GROUP_DIM))],
        out_specs=[tile, tile, tile, tile],
        out_shape=[jax.ShapeDtypeStruct((t, d), _BF16)] * 4,
        scratch_shapes=[pltpu.VMEM((MIX_TILE, d), _F32), pltpu.VMEM((MIX_TILE, d), _BF16),
                        pltpu.VMEM((MIX_TILE, d), _BF16)],
        compiler_params=_params(),
        name="mix_in",
    )(x, g, win, gv, ws, bs, wa, dft)


def _fft_a_kernel(xr_ref, xi_ref, f_ref, tc_ref, ts_ref, br_ref, bi_ref):
    n1 = FFT_N1
    f_top, f_bot = f_ref[:n1, :], f_ref[n1:, :]
    xr = jnp.swapaxes(xr_ref[...], 0, 1)
    xi = jnp.swapaxes(xi_ref[...], 0, 1)
    br, bi = [], []
    for t in range(FFT_S2_TILE):
        tc = tc_ref[:, t:t + 1]
        ts = ts_ref[:, t:t + 1]
        m = jnp.concatenate([tc * f_top + ts * f_bot, tc * f_bot - ts * f_top], axis=0)
        b = _dot(m.astype(_BF16), jnp.concatenate([xr[t], xi[t]], axis=0))
        br.append(b[:n1].astype(_BF16))
        bi.append(b[n1:].astype(_BF16))
    br_ref[...] = jnp.swapaxes(jnp.stack(br), 0, 1)
    bi_ref[...] = jnp.swapaxes(jnp.stack(bi), 0, 1)


def _fft_a(xr, xi, fmat, tw_c, tw_s, batch, seq):
    d = D_MODEL
    n1, n2 = FFT_N1, seq // FFT_N1
    m = FFT_S2_TILE
    view = (batch, n1, n2, d)
    blk = pl.BlockSpec((None, n1, m, d), lambda b, j: (b, 0, j, 0))
    tw = pl.BlockSpec((None, n1, m), lambda b, j: (j, 0, 0))
    br, bi = pl.pallas_call(
        _fft_a_kernel,
        grid=(batch, n2 // m),
        in_specs=[blk, blk, _resident((2 * n1, 2 * n1)), tw, tw],
        out_specs=[blk, blk],
        out_shape=[jax.ShapeDtypeStruct(view, _BF16), jax.ShapeDtypeStruct(view, _BF16)],
        compiler_params=_params(2),
        name="fft_a",
    )(xr.reshape(view), xi.reshape(view), fmat, tw_c, tw_s)
    return br, bi


def _fft_b_kernel(br_ref, bi_ref, g_ref, f_ref, *, n2):
    gmat = g_ref[...].astype(_BF16)
    n_slabs = br_ref.shape[0] // CHUNK
    ys = []
    for t in range(n_slabs):
        rows = slice(t * CHUNK, (t + 1) * CHUNK)
        b = jnp.concatenate([br_ref[rows, :], bi_ref[rows, :]], axis=0)
        y = _dot(gmat, b).astype(_BF16)
        ys.append(y.reshape(CHUNK // n2, n2, y.shape[-1]))
    f_ref[...] = jnp.swapaxes(jnp.concatenate(ys, axis=0), 0, 1)


def _fft_b(br, bi, gmat, batch, seq):
    d = D_MODEL
    n1, n2 = FFT_N1, seq // FFT_N1
    k1s = FFT_B_ROWS // n2
    in_view = (batch, seq, d)
    out_view = (batch, n2, n1, d)
    in_blk = pl.BlockSpec((None, FFT_B_ROWS, d), lambda b, j: (b, j, 0))
    f = pl.pallas_call(
        functools.partial(_fft_b_kernel, n2=n2),
        grid=(batch, n1 // k1s),
        in_specs=[in_blk, in_blk, _resident((CHUNK, 2 * CHUNK))],
        out_specs=pl.BlockSpec((None, n2, k1s, d), lambda b, j: (b, 0, j, 0)),
        out_shape=jax.ShapeDtypeStruct(out_view, _BF16),
        compiler_params=_params(2),
        name="fft_b",
    )(br.reshape(in_view), bi.reshape(in_view), gmat)
    return f.reshape(batch * seq, d)


def _mix_out_ffn_kernel(x_ref, gaya_ref, gb_ref, f_ref, wb_ref, wo_ref, g_ref, wg_ref, wu_ref,
                        wd_ref, fg_ref, o_ref, a_ref, *, final):
    y_b = _dot(f_ref[...], wb_ref[...])
    merged = (gaya_ref[...].astype(_F32) + gb_ref[...].astype(_F32) * y_b).astype(_BF16)
    x = x_ref[...] + _dot(merged, wo_ref[...])
    out = _swiglu_residual(x, g_ref, wg_ref, wu_ref, wd_ref, a_ref)
    if final:
        out = _rms(out, fg_ref[...])
    o_ref[...] = out


def _mix_out_ffn(x, gaya, gb, f, wb, wo, g, wg, wu, wd, fg, final):
    t = x.shape[0]
    d = D_MODEL
    tile = pl.BlockSpec((MIX_TILE, d), lambda i: (i, 0))
    return pl.pallas_call(
        functools.partial(_mix_out_ffn_kernel, final=final),
        grid=(t // MIX_TILE,),
        in_specs=[tile, tile, tile, tile, _resident((d, d)), _resident((d, d))] + _ffn_specs()
                 + [_resident((1, d))],
        out_specs=tile,
        out_shape=jax.ShapeDtypeStruct((t, d), _F32),
        scratch_shapes=[pltpu.VMEM((MIX_TILE, D_FF), _BF16)],
        compiler_params=_params(),
        name="mix_out_ffn",
    )(x, gaya, gb, f, wb, wo, g, wg, wu, wd, fg)


def _cos_sin(num, den):
    ang = 2.0 * np.pi * (np.asarray(num, dtype=np.int64) % den).astype(np.float64) / den
    return np.cos(ang), np.sin(ang)


def _channel_dft():
    n = B_GROUP_DIM
    c, s = _cos_sin(np.outer(np.arange(n), np.arange(n)), n)
    return jnp.asarray(np.concatenate([c, -s], axis=1) / np.sqrt(n), dtype=_F32)


def _position_dft(seq):
    n1, n2 = FFT_N1, seq // FFT_N1
    c1, s1 = _cos_sin(np.outer(np.arange(n1), np.arange(n1)), n1)
    fmat = np.block([[c1, s1], [-s1, c1]]) / np.sqrt(seq)
    tc, ts = _cos_sin(np.outer(np.arange(n1), np.arange(n2)), seq)
    m = FFT_S2_TILE
    tc = tc.reshape(n1, n2 // m, m).transpose(1, 0, 2)
    ts = ts.reshape(n1, n2 // m, m).transpose(1, 0, 2)
    c2, s2 = _cos_sin(np.outer(np.arange(n2), np.arange(n2)), n2)
    eye = np.eye(CHUNK // n2)
    gmat = np.concatenate([np.kron(eye, c2), np.kron(eye, s2)], axis=1)
    return tuple(jnp.asarray(t, dtype=_F32) for t in (fmat, tc, ts, gmat))


def kernel(x_prompt, x_sample, ffn1_norm, ffn1_w_gate, ffn1_w_up, ffn1_w_down, mix_norm, w_in,
           sgu_norm, sgu_w, sgu_b, w_branch_a, w_branch_b, w_out, ffn2_norm, ffn2_w_gate,
           ffn2_w_up, ffn2_w_down, final_norm):
    depth = w_in.shape[0]
    d = D_MODEL
    bf = lambda w: w.astype(_BF16)
    row = lambda v: v.reshape(1, d).astype(_F32)
    dft = _channel_dft()
    bias = jnp.repeat(jnp.swapaxes(sgu_b, 1, 2), A_HEAD_DIM, axis=2).astype(_F32)
    weights = dict(
        wg1=bf(ffn1_w_gate), wu1=bf(ffn1_w_up), wd1=bf(ffn1_w_down), win=bf(w_in), ws=bf(sgu_w),
        wa=bf(w_branch_a), wb=bf(w_branch_b), wo=bf(w_out), wg2=bf(ffn2_w_gate),
        wu2=bf(ffn2_w_up), wd2=bf(ffn2_w_down))
    fg = row(final_norm)

    def trunk(x3):
        batch, seq, _ = x3.shape
        fmat, tw_c, tw_s, gmat = _position_dft(seq)
        x = x3.reshape(batch * seq, d)
        for l in range(depth):
            w = {k: v[l] for k, v in weights.items()}
            x = _ffn(x, row(ffn1_norm[l]), w["wg1"], w["wu1"], w["wd1"])
            gaya, gb, xr, xi = _mix_in(x, row(mix_norm[l]), w["win"], row(sgu_norm[l]), w["ws"],
                                       bias[l], w["wa"], dft)
            br, bi = _fft_a(xr, xi, fmat, tw_c, tw_s, batch, seq)
            f = _fft_b(br, bi, gmat, batch, seq)
            x = _mix_out_ffn(x, gaya, gb, f, w["wb"], w["wo"], row(ffn2_norm[l]), w["wg2"],
                             w["wu2"], w["wd2"], fg, final=(l == depth - 1))
        return x.reshape(batch, seq, d)

    return (trunk(x_prompt), trunk(x_sample))
```

```python
import functools

import numpy as np
import jax
import jax.numpy as jnp
from jax import lax
from jax.experimental import pallas as pl
from jax.experimental.pallas import tpu as pltpu

D_MODEL = 1024
CHUNK = 128
A_HEADS = 8
A_HEAD_DIM = D_MODEL // A_HEADS
B_GROUPS = 4
B_GROUP_DIM = D_MODEL // B_GROUPS
D_FF = 2816
EPS = 1e-6

FFN_TILE = 1024
MIX_TILE = 1024
MIX_SUB = 512
FF_COLS = 256
FFT_N1 = 128
FFT_S2_TILE = 16
FFT_B_ROWS = 2048
VMEM_LIMIT_BYTES = 56 * 1024 * 1024

_BF16 = jnp.bfloat16
_F32 = jnp.float32


def _dot(a, b):
    return jnp.dot(a, b, preferred_element_type=_F32)


def _rms(x, g):
    return x * lax.rsqrt(jnp.mean(x * x, axis=-1, keepdims=True) + EPS) * g


def _gelu(x):
    return 0.5 * x * (1.0 + lax.erf(x * np.float32(np.sqrt(0.5))))


def _resident(shape):
    return pl.BlockSpec(shape, lambda *_: (0,) * len(shape), pipeline_mode=pl.Buffered(1))


def _params(n_grid_dims=1):
    return pltpu.CompilerParams(dimension_semantics=("arbitrary",) * n_grid_dims,
                                vmem_limit_bytes=VMEM_LIMIT_BYTES)


def _swiglu_residual(x, g_ref, wg_ref, wu_ref, wd_ref, a_ref):
    h = _rms(x, g_ref[...]).astype(_BF16)
    for c in range(0, D_FF, FF_COLS):
        gate = _dot(h, wg_ref[:, c:c + FF_COLS])
        up = _dot(h, wu_ref[:, c:c + FF_COLS])
        a_ref[:, c:c + FF_COLS] = (jax.nn.silu(gate) * up).astype(_BF16)
    return x + 0.5 * _dot(a_ref[...], wd_ref[...])


def _ffn_kernel(x_ref, g_ref, wg_ref, wu_ref, wd_ref, o_ref, a_ref):
    o_ref[...] = _swiglu_residual(x_ref[...], g_ref, wg_ref, wu_ref, wd_ref, a_ref)


def _ffn_specs():
    return [_resident((1, D_MODEL)), _resident((D_MODEL, D_FF)), _resident((D_MODEL, D_FF)),
            _resident((D_FF, D_MODEL))]


def _ffn(x, g, wg, wu, wd):
    t = x.shape[0]
    tile = pl.BlockSpec((FFN_TILE, D_MODEL), lambda i: (i, 0))
    return pl.pallas_call(
        _ffn_kernel,
        grid=(t // FFN_TILE,),
        in_specs=[tile] + _ffn_specs(),
        out_specs=tile,
        out_shape=jax.ShapeDtypeStruct((t, D_MODEL), _F32),
        scratch_shapes=[pltpu.VMEM((FFN_TILE, D_FF), _BF16)],
        compiler_params=_params(),
        name="ffn",
    )(x, g, wg, wu, wd)


def _mix_in_kernel(x_ref, g_ref, win_ref, gv_ref, ws_ref, bs_ref, wa_ref, dft_ref,
                   gaya_ref, gb_ref, xr_ref, xi_ref, u_scr, v_scr, sg_scr):
    d = D_MODEL
    gd = B_GROUP_DIM
    dft = dft_ref[...].astype(_BF16)
    n_chunks = MIX_SUB // CHUNK
    for s in range(MIX_TILE // MIX_SUB):
        tile_rows = slice(s * MIX_SUB, (s + 1) * MIX_SUB)
        h = _rms(x_ref[tile_rows, :], g_ref[...]).astype(_BF16)
        u_scr[...] = _gelu(_dot(h, win_ref[:, 0:d]))
        v = _gelu(_dot(h, win_ref[:, d:2 * d]))
        v_scr[...] = _rms(v, gv_ref[...]).astype(_BF16)

        for hd in range(A_HEADS):
            lanes = slice(hd * A_HEAD_DIM, (hd + 1) * A_HEAD_DIM)
            rhs = jnp.concatenate(
                [v_scr[c * CHUNK:(c + 1) * CHUNK, lanes] for c in range(n_chunks)], axis=1)
            mixed = _dot(ws_ref[hd], rhs)
            for c in range(n_chunks):
                rows = slice(c * CHUNK, (c + 1) * CHUNK)
                sp = mixed[:, c * A_HEAD_DIM:(c + 1) * A_HEAD_DIM] + bs_ref[:, lanes]
                sg_scr[rows, lanes] = (u_scr[rows, lanes] * sp).astype(_BF16)

        y_a = _dot(sg_scr[...], wa_ref[...])
        gate_a = jax.nn.sigmoid(_dot(h, win_ref[:, 3 * d:4 * d]))
        gaya_ref[tile_rows, :] = (gate_a * y_a).astype(_BF16)
        gb_ref[tile_rows, :] = jax.nn.sigmoid(_dot(h, win_ref[:, 4 * d:5 * d])).astype(_BF16)

        z_b = _dot(h, win_ref[:, 2 * d:3 * d]).astype(_BF16)
        for grp in range(B_GROUPS):
            cols = slice(grp * gd, (grp + 1) * gd)
            spec = _dot(z_b[:, cols], dft)
            xr_ref[tile_rows, cols] = spec[:, :gd].astype(_BF16)
            xi_ref[tile_rows, cols] = spec[:, gd:].astype(_BF16)


def _mix_in(x, g, win, gv, ws, bs, wa, dft):
    t = x.shape[0]
    d = D_MODEL
    tile = pl.BlockSpec((MIX_TILE, d), lambda i: (i, 0))
    return pl.pallas_call(
        _mix_in_kernel,
        grid=(t // MIX_TILE,),
        in_specs=[tile, _resident((1, d)), _resident((d, 5 * d)), _resident((1, d)),
                  _resident((A_HEADS, CHUNK, CHUNK)), _resident((CHUNK, d)),
                  _resident((d, d)), _resident((B_GROUP_DIM, 2 * B_GROUP_DIM))],
        out_specs=[tile, tile, tile, tile],
        out_shape=[jax.ShapeDtypeStruct((t, d), _BF16)] * 4,
        scratch_shapes=[pltpu.VMEM((MIX_SUB, d), _F32), pltpu.VMEM((MIX_SUB, d), _BF16),
                        pltpu.VMEM((MIX_SUB, d), _BF16)],
        compiler_params=_params(),
        name="mix_in",
    )(x, g, win, gv, ws, bs, wa, dft)


def _fft_a_kernel(xr_ref, xi_ref, f_ref, tc_ref, ts_ref, br_ref, bi_ref):
    n1 = FFT_N1
    f_top, f_bot = f_ref[:n1, :], f_ref[n1:, :]
    xr = jnp.swapaxes(xr_ref[...], 0, 1)
    xi = jnp.swapaxes(xi_ref[...], 0, 1)
    br, bi = [], []
    for t in range(FFT_S2_TILE):
        tc = tc_ref[:, t:t + 1]
        ts = ts_ref[:, t:t + 1]
        m = jnp.concatenate([tc * f_top + ts * f_bot, tc * f_bot - ts * f_top], axis=0)
        b = _dot(m.astype(_BF16), jnp.concatenate([xr[t], xi[t]], axis=0))
        br.append(b[:n1].astype(_BF16))
        bi.append(b[n1:].astype(_BF16))
    br_ref[...] = jnp.swapaxes(jnp.stack(br), 0, 1)
    bi_ref[...] = jnp.swapaxes(jnp.stack(bi), 0, 1)


def _fft_a(xr, xi, fmat, tw_c, tw_s, batch, seq):
    d = D_MODEL
    n1, n2 = FFT_N1, seq // FFT_N1
    m = FFT_S2_TILE
    view = (batch, n1, n2, d)
    blk = pl.BlockSpec((None, n1, m, d), lambda b, j: (b, 0, j, 0))
    tw = pl.BlockSpec((None, n1, m), lambda b, j: (j, 0, 0))
    br, bi = pl.pallas_call(
        _fft_a_kernel,
        grid=(batch, n2 // m),
        in_specs=[blk, blk, _resident((2 * n1, 2 * n1)), tw, tw],
        out_specs=[blk, blk],
        out_shape=[jax.ShapeDtypeStruct(view, _BF16), jax.ShapeDtypeStruct(view, _BF16)],
        compiler_params=_params(2),
        name="fft_a",
    )(xr.reshape(view), xi.reshape(view), fmat, tw_c, tw_s)
    return br, bi


def _fft_b_kernel(br_ref, bi_ref, g_ref, f_ref, *, n2):
    gmat = g_ref[...].astype(_BF16)
    n_slabs = br_ref.shape[0] // CHUNK
    ys = []
    for t in range(n_slabs):
        rows = slice(t * CHUNK, (t + 1) * CHUNK)
        b = jnp.concatenate([br_ref[rows, :], bi_ref[rows, :]], axis=0)
        y = _dot(gmat, b).astype(_BF16)
        ys.append(y.reshape(CHUNK // n2, n2, y.shape[-1]))
    f_ref[...] = jnp.swapaxes(jnp.concatenate(ys, axis=0), 0, 1)


def _fft_b(br, bi, gmat, batch, seq):
    d = D_MODEL
    n1, n2 = FFT_N1, seq // FFT_N1
    k1s = FFT_B_ROWS // n2
    in_view = (batch, seq, d)
    out_view = (batch, n2, n1, d)
    in_blk = pl.BlockSpec((None, FFT_B_ROWS, d), lambda b, j: (b, j, 0))
    f = pl.pallas_call(
        functools.partial(_fft_b_kernel, n2=n2),
        grid=(batch, n1 // k1s),
        in_specs=[in_blk, in_blk, _resident((CHUNK, 2 * CHUNK))],
        out_specs=pl.BlockSpec((None, n2, k1s, d), lambda b, j: (b, 0, j, 0)),
        out_shape=jax.ShapeDtypeStruct(out_view, _BF16),
        compiler_params=_params(2),
        name="fft_b",
    )(br.reshape(in_view), bi.reshape(in_view), gmat)
    return f.reshape(batch * seq, d)


def _mix_out_ffn_kernel(x_ref, gaya_ref, gb_ref, f_ref, wb_ref, wo_ref, g_ref, wg_ref, wu_ref,
                        wd_ref, fg_ref, o_ref, a_ref, *, final):
    for s in range(MIX_TILE // MIX_SUB):
        rows = slice(s * MIX_SUB, (s + 1) * MIX_SUB)
        y_b = _dot(f_ref[rows, :], wb_ref[...])
        gated = gaya_ref[rows, :].astype(_F32) + gb_ref[rows, :].astype(_F32) * y_b
        x = x_ref[rows, :] + _dot(gated.astype(_BF16), wo_ref[...])
        out = _swiglu_residual(x, g_ref, wg_ref, wu_ref, wd_ref, a_ref)
        if final:
            out = _rms(out, fg_ref[...])
        o_ref[rows, :] = out


def _mix_out_ffn(x, gaya, gb, f, wb, wo, g, wg, wu, wd, fg, final):
    t = x.shape[0]
    d = D_MODEL
    tile = pl.BlockSpec((MIX_TILE, d), lambda i: (i, 0))
    return pl.pallas_call(
        functools.partial(_mix_out_ffn_kernel, final=final),
        grid=(t // MIX_TILE,),
        in_specs=[tile, tile, tile, tile, _resident((d, d)), _resident((d, d))] + _ffn_specs()
                 + [_resident((1, d))],
        out_specs=tile,
        out_shape=jax.ShapeDtypeStruct((t, d), _F32),
        scratch_shapes=[pltpu.VMEM((MIX_SUB, D_FF), _BF16)],
        compiler_params=_params(),
        name="mix_out_ffn",
    )(x, gaya, gb, f, wb, wo, g, wg, wu, wd, fg)


def _cos_sin(num, den):
    ang = 2.0 * np.pi * (np.asarray(num, dtype=np.int64) % den).astype(np.float64) / den
    return np.cos(ang), np.sin(ang)


def _channel_dft():
    n = B_GROUP_DIM
    c, s = _cos_sin(np.outer(np.arange(n), np.arange(n)), n)
    return jnp.asarray(np.concatenate([c, -s], axis=1) / np.sqrt(n), dtype=_F32)


def _position_dft(seq):
    n1, n2 = FFT_N1, seq // FFT_N1
    c1, s1 = _cos_sin(np.outer(np.arange(n1), np.arange(n1)), n1)
    fmat = np.block([[c1, s1], [-s1, c1]]) / np.sqrt(seq)
    tc, ts = _cos_sin(np.outer(np.arange(n1), np.arange(n2)), seq)
    m = FFT_S2_TILE
    tc = tc.reshape(n1, n2 // m, m).transpose(1, 0, 2)
    ts = ts.reshape(n1, n2 // m, m).transpose(1, 0, 2)
    c2, s2 = _cos_sin(np.outer(np.arange(n2), np.arange(n2)), n2)
    eye = np.eye(CHUNK // n2)
    gmat = np.concatenate([np.kron(eye, c2), np.kron(eye, s2)], axis=1)
    return tuple(jnp.asarray(t, dtype=_F32) for t in (fmat, tc, ts, gmat))


def kernel(x_prompt, x_sample, ffn1_norm, ffn1_w_gate, ffn1_w_up, ffn1_w_down, mix_norm, w_in,
           sgu_norm, sgu_w, sgu_b, w_branch_a, w_branch_b, w_out, ffn2_norm, ffn2_w_gate,
           ffn2_w_up, ffn2_w_down, final_norm):
    depth = w_in.shape[0]
    d = D_MODEL
    bf = lambda w: w.astype(_BF16)
    row = lambda v: v.reshape(1, d).astype(_F32)
    dft = _channel_dft()
    bias = jnp.repeat(jnp.swapaxes(sgu_b, 1, 2), A_HEAD_DIM, axis=2).astype(_F32)
    weights = dict(
        wg1=bf(ffn1_w_gate), wu1=bf(ffn1_w_up), wd1=bf(ffn1_w_down), win=bf(w_in), ws=bf(sgu_w),
        wa=bf(w_branch_a), wb=bf(w_branch_b), wo=bf(w_out), wg2=bf(ffn2_w_gate),
        wu2=bf(ffn2_w_up), wd2=bf(ffn2_w_down))
    fg = row(final_norm)

    def trunk(x3):
        batch, seq, _ = x3.shape
        fmat, tw_c, tw_s, gmat = _position_dft(seq)
        x = x3.reshape(batch * seq, d)
        for l in range(depth):
            w = {k: v[l] for k, v in weights.items()}
            x = _ffn(x, row(ffn1_norm[l]), w["wg1"], w["wu1"], w["wd1"])
            gaya, gb, xr, xi = _mix_in(x, row(mix_norm[l]), w["win"], row(sgu_norm[l]), w["ws"],
                                       bias[l], w["wa"], dft)
            br, bi = _fft_a(xr, xi, fmat, tw_c, tw_s, batch, seq)
            f = _fft_b(br, bi, gmat, batch, seq)
            x = _mix_out_ffn(x, gaya, gb, f, w["wb"], w["wo"], row(ffn2_norm[l]), w["wg2"],
                             w["wu2"], w["wd2"], fg, final=(l == depth - 1))
        return x.reshape(batch, seq, d)

    return (trunk(x_prompt), trunk(x_sample))
```

```python
import functools

import numpy as np
import jax
import jax.numpy as jnp
from jax import lax
from jax.experimental import pallas as pl
from jax.experimental.pallas import tpu as pltpu

D_MODEL = 1024
CHUNK = 128
A_HEADS = 8
A_HEAD_DIM = D_MODEL // A_HEADS
B_GROUPS = 4
B_GROUP_DIM = D_MODEL // B_GROUPS
D_FF = 2816
EPS = 1e-6

FFN_TILE = 1024
MIX_TILE = 1024
MIX_IN_SUB = 256
MIX_SUB = 512
FF_COLS = 256
FFT_N1 = 128
FFT_S2_TILE = 16
FFT_B_ROWS = 2048
VMEM_LIMIT_BYTES = 56 * 1024 * 1024

_BF16 = jnp.bfloat16
_F32 = jnp.float32


def _dot(a, b):
    return jnp.dot(a, b, preferred_element_type=_F32)


def _rms(x, g):
    return x * lax.rsqrt(jnp.mean(x * x, axis=-1, keepdims=True) + EPS) * g


def _gelu(x):
    return 0.5 * x * (1.0 + lax.erf(x * np.float32(np.sqrt(0.5))))


def _resident(shape):
    return pl.BlockSpec(shape, lambda *_: (0,) * len(shape), pipeline_mode=pl.Buffered(1))


def _params(n_grid_dims=1):
    return pltpu.CompilerParams(dimension_semantics=("arbitrary",) * n_grid_dims,
                                vmem_limit_bytes=VMEM_LIMIT_BYTES)


def _swiglu_residual(x, g_ref, wg_ref, wu_ref, wd_ref, a_ref):
    h = _rms(x, g_ref[...]).astype(_BF16)
    for c in range(0, D_FF, FF_COLS):
        gate = _dot(h, wg_ref[:, c:c + FF_COLS])
        up = _dot(h, wu_ref[:, c:c + FF_COLS])
        a_ref[:, c:c + FF_COLS] = (jax.nn.silu(gate) * up).astype(_BF16)
    return x + 0.5 * _dot(a_ref[...], wd_ref[...])


def _ffn_kernel(x_ref, g_ref, wg_ref, wu_ref, wd_ref, o_ref, a_ref):
    o_ref[...] = _swiglu_residual(x_ref[...], g_ref, wg_ref, wu_ref, wd_ref, a_ref)


def _ffn_specs():
    return [_resident((1, D_MODEL)), _resident((D_MODEL, D_FF)), _resident((D_MODEL, D_FF)),
            _resident((D_FF, D_MODEL))]


def _ffn(x, g, wg, wu, wd):
    t = x.shape[0]
    tile = pl.BlockSpec((FFN_TILE, D_MODEL), lambda i: (i, 0))
    return pl.pallas_call(
        _ffn_kernel,
        grid=(t // FFN_TILE,),
        in_specs=[tile] + _ffn_specs(),
        out_specs=tile,
        out_shape=jax.ShapeDtypeStruct((t, D_MODEL), _F32),
        scratch_shapes=[pltpu.VMEM((FFN_TILE, D_FF), _BF16)],
        compiler_params=_params(),
        name="ffn",
    )(x, g, wg, wu, wd)


def _mix_in_kernel(x_ref, g_ref, win_ref, gv_ref, ws_ref, bs_ref, wa_ref, dft_ref,
                   gaya_ref, gb_ref, xr_ref, xi_ref, u_scr, v_scr, sg_scr):
    d = D_MODEL
    gd = B_GROUP_DIM
    dft = dft_ref[...].astype(_BF16)
    n_chunks = MIX_IN_SUB // CHUNK
    n_pass = MIX_TILE // MIX_IN_SUB
    rows_of = [slice(s * MIX_IN_SUB, (s + 1) * MIX_IN_SUB) for s in range(n_pass)]

    def front(s):
        h = _rms(x_ref[rows_of[s], :], g_ref[...]).astype(_BF16)
        u_scr[s] = _gelu(_dot(h, win_ref[:, 0:d]))
        return h, _gelu(_dot(h, win_ref[:, d:2 * d]))

    def gate_b(s, h):
        gb_ref[rows_of[s], :] = jax.nn.sigmoid(_dot(h, win_ref[:, 4 * d:5 * d])).astype(_BF16)

    def spatial_gate(s):
        for hd in range(A_HEADS):
            lanes = slice(hd * A_HEAD_DIM, (hd + 1) * A_HEAD_DIM)
            rhs = jnp.concatenate(
                [v_scr[s, c * CHUNK:(c + 1) * CHUNK, lanes] for c in range(n_chunks)], axis=1)
            mixed = _dot(ws_ref[hd], rhs)
            for c in range(n_chunks):
                rows = slice(c * CHUNK, (c + 1) * CHUNK)
                sp = mixed[:, c * A_HEAD_DIM:(c + 1) * A_HEAD_DIM] + bs_ref[:, lanes]
                sg_scr[s, rows, lanes] = (u_scr[s, rows, lanes] * sp).astype(_BF16)

    def branch_a(s, h):
        y_a = _dot(sg_scr[s], wa_ref[...])
        gate_a = jax.nn.sigmoid(_dot(h, win_ref[:, 3 * d:4 * d]))
        gaya_ref[rows_of[s], :] = (gate_a * y_a).astype(_BF16)

    def fourier(s, h):
        z_b = _dot(h, win_ref[:, 2 * d:3 * d]).astype(_BF16)
        for grp in range(B_GROUPS):
            cols = slice(grp * gd, (grp + 1) * gd)
            spec = _dot(z_b[:, cols], dft)
            xr_ref[rows_of[s], cols] = spec[:, :gd].astype(_BF16)
            xi_ref[rows_of[s], cols] = spec[:, gd:].astype(_BF16)

    h, v = front(0)
    for s in range(n_pass):
        gate_b(s, h)
        v_scr[s] = _rms(v, gv_ref[...]).astype(_BF16)
        if s + 1 < n_pass:
            h_next, v = front(s + 1)
        spatial_gate(s)
        branch_a(s, h)
        fourier(s, h)
        if s + 1 < n_pass:
            h = h_next


def _mix_in(x, g, win, gv, ws, bs, wa, dft):
    t = x.shape[0]
    d = D_MODEL
    tile = pl.BlockSpec((MIX_TILE, d), lambda i: (i, 0))
    return pl.pallas_call(
        _mix_in_kernel,
        grid=(t // MIX_TILE,),
        in_specs=[tile, _resident((1, d)), _resident((d, 5 * d)), _resident((1, d)),
                  _resident((A_HEADS, CHUNK, CHUNK)), _resident((CHUNK, d)),
                  _resident((d, d)), _resident((B_GROUP_DIM, 2 * B_GROUP_DIM))],
        out_specs=[tile, tile, tile, tile],
        out_shape=[jax.ShapeDtypeStruct((t, d), _BF16)] * 4,
        scratch_shapes=[pltpu.VMEM((MIX_TILE // MIX_IN_SUB, MIX_IN_SUB, d), dt)
                        for dt in (_F32, _BF16, _BF16)],
        compiler_params=_params(),
        name="mix_in",
    )(x, g, win, gv, ws, bs, wa, dft)


def _fft_a_kernel(xr_ref, xi_ref, f_ref, tc_ref, ts_ref, br_ref, bi_ref):
    n1 = FFT_N1
    f_top, f_bot = f_ref[:n1, :], f_ref[n1:, :]
    xr = jnp.swapaxes(xr_ref[...], 0, 1)
    xi = jnp.swapaxes(xi_ref[...], 0, 1)
    for t in range(FFT_S2_TILE):
        tc = tc_ref[:, t:t + 1]
        ts = ts_ref[:, t:t + 1]
        m = jnp.concatenate([tc * f_top + ts * f_bot, tc * f_bot - ts * f_top], axis=0)
        b = _dot(m.astype(_BF16), jnp.concatenate([xr[t], xi[t]], axis=0))
        br_ref[t] = b[:n1].astype(_BF16)
        bi_ref[t] = b[n1:].astype(_BF16)


def _fft_a(xr, xi, fmat, tw_c, tw_s, batch, seq):
    d = D_MODEL
    n1, n2 = FFT_N1, seq // FFT_N1
    m = FFT_S2_TILE
    in_view = (batch, n1, n2, d)
    out_view = (batch, n2, n1, d)
    in_blk = pl.BlockSpec((None, n1, m, d), lambda b, j: (b, 0, j, 0))
    out_blk = pl.BlockSpec((None, m, n1, d), lambda b, j: (b, j, 0, 0))
    tw = pl.BlockSpec((None, n1, m), lambda b, j: (j, 0, 0))
    return pl.pallas_call(
        _fft_a_kernel,
        grid=(batch, n2 // m),
        in_specs=[in_blk, in_blk, _resident((2 * n1, 2 * n1)), tw, tw],
        out_specs=[out_blk, out_blk],
        out_shape=[jax.ShapeDtypeStruct(out_view, _BF16)] * 2,
        compiler_params=_params(2),
        name="fft_a",
    )(xr.reshape(in_view), xi.reshape(in_view), fmat, tw_c, tw_s)


def _fft_b_kernel(br_ref, bi_ref, g_ref, f_ref, *, n2):
    gmat = g_ref[...].astype(_BF16)
    d = br_ref.shape[-1]
    n_slabs = br_ref.shape[1] * n2 // CHUNK
    br = jnp.swapaxes(br_ref[...], 0, 1).reshape(n_slabs, CHUNK, d)
    bi = jnp.swapaxes(bi_ref[...], 0, 1).reshape(n_slabs, CHUNK, d)
    ys = []
    for t in range(n_slabs):
        y = _dot(gmat, jnp.concatenate([br[t], bi[t]], axis=0)).astype(_BF16)
        ys.append(y.reshape(CHUNK // n2, n2, d))
    f_ref[...] = jnp.swapaxes(jnp.concatenate(ys, axis=0), 0, 1)


def _fft_b(br, bi, gmat, batch, seq):
    d = D_MODEL
    n1, n2 = FFT_N1, seq // FFT_N1
    k1s = FFT_B_ROWS // n2
    blk = pl.BlockSpec((None, n2, k1s, d), lambda b, j: (b, 0, j, 0))
    f = pl.pallas_call(
        functools.partial(_fft_b_kernel, n2=n2),
        grid=(batch, n1 // k1s),
        in_specs=[blk, blk, _resident((CHUNK, 2 * CHUNK))],
        out_specs=blk,
        out_shape=jax.ShapeDtypeStruct((batch, n2, n1, d), _BF16),
        compiler_params=_params(2),
        name="fft_b",
    )(br, bi, gmat)
    return f.reshape(batch * seq, d)


def _mix_out_ffn_kernel(x_ref, gaya_ref, gb_ref, f_ref, wb_ref, wo_ref, g_ref, wg_ref, wu_ref,
                        wd_ref, fg_ref, o_ref, a_ref, *, final):
    for s in range(MIX_TILE // MIX_SUB):
        rows = slice(s * MIX_SUB, (s + 1) * MIX_SUB)
        y_b = _dot(f_ref[rows, :], wb_ref[...])
        gated = gaya_ref[rows, :].astype(_F32) + gb_ref[rows, :].astype(_F32) * y_b
        x = x_ref[rows, :] + _dot(gated.astype(_BF16), wo_ref[...])
        out = _swiglu_residual(x, g_ref, wg_ref, wu_ref, wd_ref, a_ref)
        if final:
            out = _rms(out, fg_ref[...])
        o_ref[rows, :] = out


def _mix_out_ffn(x, gaya, gb, f, wb, wo, g, wg, wu, wd, fg, final):
    t = x.shape[0]
    d = D_MODEL
    tile = pl.BlockSpec((MIX_TILE, d), lambda i: (i, 0))
    return pl.pallas_call(
        functools.partial(_mix_out_ffn_kernel, final=final),
        grid=(t // MIX_TILE,),
        in_specs=[tile, tile, tile, tile, _resident((d, d)), _resident((d, d))] + _ffn_specs()
                 + [_resident((1, d))],
        out_specs=tile,
        out_shape=jax.ShapeDtypeStruct((t, d), _F32),
        scratch_shapes=[pltpu.VMEM((MIX_SUB, D_FF), _BF16)],
        compiler_params=_params(),
        name="mix_out_ffn",
    )(x, gaya, gb, f, wb, wo, g, wg, wu, wd, fg)


def _cos_sin(num, den):
    ang = 2.0 * np.pi * (np.asarray(num, dtype=np.int64) % den).astype(np.float64) / den
    return np.cos(ang), np.sin(ang)


def _channel_dft():
    n = B_GROUP_DIM
    c, s = _cos_sin(np.outer(np.arange(n), np.arange(n)), n)
    return jnp.asarray(np.concatenate([c, -s], axis=1) / np.sqrt(n), dtype=_F32)


def _position_dft(seq):
    n1, n2 = FFT_N1, seq // FFT_N1
    c1, s1 = _cos_sin(np.outer(np.arange(n1), np.arange(n1)), n1)
    fmat = np.block([[c1, s1], [-s1, c1]]) / np.sqrt(seq)
    tc, ts = _cos_sin(np.outer(np.arange(n1), np.arange(n2)), seq)
    m = FFT_S2_TILE
    tc = tc.reshape(n1, n2 // m, m).transpose(1, 0, 2)
    ts = ts.reshape(n1, n2 // m, m).transpose(1, 0, 2)
    c2, s2 = _cos_sin(np.outer(np.arange(n2), np.arange(n2)), n2)
    eye = np.eye(CHUNK // n2)
    gmat = np.concatenate([np.kron(eye, c2), np.kron(eye, s2)], axis=1)
    return tuple(jnp.asarray(t, dtype=_F32) for t in (fmat, tc, ts, gmat))


def kernel(x_prompt, x_sample, ffn1_norm, ffn1_w_gate, ffn1_w_up, ffn1_w_down, mix_norm, w_in,
           sgu_norm, sgu_w, sgu_b, w_branch_a, w_branch_b, w_out, ffn2_norm, ffn2_w_gate,
           ffn2_w_up, ffn2_w_down, final_norm):
    depth = w_in.shape[0]
    d = D_MODEL
    bf = lambda w: w.astype(_BF16)
    row = lambda v: v.reshape(1, d).astype(_F32)
    dft = _channel_dft()
    bias = jnp.repeat(jnp.swapaxes(sgu_b, 1, 2), A_HEAD_DIM, axis=2).astype(_F32)
    weights = dict(
        wg1=bf(ffn1_w_gate), wu1=bf(ffn1_w_up), wd1=bf(ffn1_w_down), win=bf(w_in), ws=bf(sgu_w),
        wa=bf(w_branch_a), wb=bf(w_branch_b), wo=bf(w_out), wg2=bf(ffn2_w_gate),
        wu2=bf(ffn2_w_up), wd2=bf(ffn2_w_down))
    fg = row(final_norm)

    def trunk(x3):
        batch, seq, _ = x3.shape
        fmat, tw_c, tw_s, gmat = _position_dft(seq)
        x = x3.reshape(batch * seq, d)
        for l in range(depth):
            w = {k: v[l] for k, v in weights.items()}
            x = _ffn(x, row(ffn1_norm[l]), w["wg1"], w["wu1"], w["wd1"])
            gaya, gb, xr, xi = _mix_in(x, row(mix_norm[l]), w["win"], row(sgu_norm[l]), w["ws"],
                                       bias[l], w["wa"], dft)
            br, bi = _fft_a(xr, xi, fmat, tw_c, tw_s, batch, seq)
            f = _fft_b(br, bi, gmat, batch, seq)
            x = _mix_out_ffn(x, gaya, gb, f, w["wb"], w["wo"], row(ffn2_norm[l]), w["wg2"],
                             w["wu2"], w["wd2"], fg, final=(l == depth - 1))
        return x.reshape(batch, seq, d)

    return (trunk(x_prompt), trunk(x_sample))
```

```python
import functools

import numpy as np
import jax
import jax.numpy as jnp
from jax import lax
from jax.experimental import pallas as pl
from jax.experimental.pallas import tpu as pltpu

D_MODEL = 1024
CHUNK = 128
A_HEADS = 8
A_HEAD_DIM = D_MODEL // A_HEADS
B_GROUPS = 4
B_GROUP_DIM = D_MODEL // B_GROUPS
D_FF = 2816
EPS = 1e-6

FFN_TILE = 1024
MIX_TILE = 1024
MIX_IN_SUB = 256
MIX_SUB = 512
FF_COLS = 256
FFT_N1 = 128
FFT_S2_TILE = 16
FFT_B_ROWS = 2048
VMEM_LIMIT_BYTES = 56 * 1024 * 1024

_BF16 = jnp.bfloat16
_F32 = jnp.float32


def _dot(a, b):
    return jnp.dot(a, b, preferred_element_type=_F32)


def _rms(x, g):
    return x * lax.rsqrt(jnp.mean(x * x, axis=-1, keepdims=True) + EPS) * g


def _gelu(x):
    return 0.5 * x * (1.0 + lax.erf(x * np.float32(np.sqrt(0.5))))


def _resident(shape):
    return pl.BlockSpec(shape, lambda *_: (0,) * len(shape), pipeline_mode=pl.Buffered(1))


def _params(n_grid_dims=1):
    return pltpu.CompilerParams(dimension_semantics=("arbitrary",) * n_grid_dims,
                                vmem_limit_bytes=VMEM_LIMIT_BYTES)


def _swiglu_residual(x, g_ref, wg_ref, wu_ref, wd_ref, a_ref):
    h = _rms(x, g_ref[...]).astype(_BF16)
    for c in range(0, D_FF, FF_COLS):
        gate = _dot(h, wg_ref[:, c:c + FF_COLS])
        up = _dot(h, wu_ref[:, c:c + FF_COLS])
        a_ref[:, c:c + FF_COLS] = (jax.nn.silu(gate) * up).astype(_BF16)
    return x + 0.5 * _dot(a_ref[...], wd_ref[...])


def _ffn_kernel(x_ref, g_ref, wg_ref, wu_ref, wd_ref, o_ref, a_ref):
    o_ref[...] = _swiglu_residual(x_ref[...], g_ref, wg_ref, wu_ref, wd_ref, a_ref)


def _ffn_specs():
    return [_resident((1, D_MODEL)), _resident((D_MODEL, D_FF)), _resident((D_MODEL, D_FF)),
            _resident((D_FF, D_MODEL))]


def _ffn(x, g, wg, wu, wd):
    t = x.shape[0]
    tile = pl.BlockSpec((FFN_TILE, D_MODEL), lambda i: (i, 0))
    return pl.pallas_call(
        _ffn_kernel,
        grid=(t // FFN_TILE,),
        in_specs=[tile] + _ffn_specs(),
        out_specs=tile,
        out_shape=jax.ShapeDtypeStruct((t, D_MODEL), _F32),
        scratch_shapes=[pltpu.VMEM((FFN_TILE, D_FF), _BF16)],
        compiler_params=_params(),
        name="ffn",
    )(x, g, wg, wu, wd)


def _mix_in_kernel(x_ref, g_ref, win_ref, gv_ref, ws_ref, bs_ref, wa_ref,
                   gaya_ref, gb_ref, zb_ref, u_scr, v_scr, sg_scr):
    d = D_MODEL
    n_chunks = MIX_IN_SUB // CHUNK
    n_pass = MIX_TILE // MIX_IN_SUB
    rows_of = [slice(s * MIX_IN_SUB, (s + 1) * MIX_IN_SUB) for s in range(n_pass)]

    def front(s):
        h = _rms(x_ref[rows_of[s], :], g_ref[...]).astype(_BF16)
        u_scr[s] = _gelu(_dot(h, win_ref[:, 0:d]))
        return h, _gelu(_dot(h, win_ref[:, d:2 * d]))

    def gate_b(s, h):
        gb_ref[rows_of[s], :] = jax.nn.sigmoid(_dot(h, win_ref[:, 4 * d:5 * d])).astype(_BF16)

    def spatial_gate(s):
        for hd in range(A_HEADS):
            lanes = slice(hd * A_HEAD_DIM, (hd + 1) * A_HEAD_DIM)
            rhs = jnp.concatenate(
                [v_scr[s, c * CHUNK:(c + 1) * CHUNK, lanes] for c in range(n_chunks)], axis=1)
            mixed = _dot(ws_ref[hd], rhs)
            for c in range(n_chunks):
                rows = slice(c * CHUNK, (c + 1) * CHUNK)
                sp = mixed[:, c * A_HEAD_DIM:(c + 1) * A_HEAD_DIM] + bs_ref[:, lanes]
                sg_scr[s, rows, lanes] = (u_scr[s, rows, lanes] * sp).astype(_BF16)

    def branch_a(s, h):
        y_a = _dot(sg_scr[s], wa_ref[...])
        gate_a = jax.nn.sigmoid(_dot(h, win_ref[:, 3 * d:4 * d]))
        gaya_ref[rows_of[s], :] = (gate_a * y_a).astype(_BF16)

    def fourier(s, h):
        zb_ref[rows_of[s], :] = _dot(h, win_ref[:, 2 * d:3 * d]).astype(_BF16)

    h, v = front(0)
    for s in range(n_pass):
        gate_b(s, h)
        v_scr[s] = _rms(v, gv_ref[...]).astype(_BF16)
        if s + 1 < n_pass:
            h_next, v = front(s + 1)
        spatial_gate(s)
        branch_a(s, h)
        fourier(s, h)
        if s + 1 < n_pass:
            h = h_next


def _mix_in(x, g, win, gv, ws, bs, wa):
    t = x.shape[0]
    d = D_MODEL
    tile = pl.BlockSpec((MIX_TILE, d), lambda i: (i, 0))
    return pl.pallas_call(
        _mix_in_kernel,
        grid=(t // MIX_TILE,),
        in_specs=[tile, _resident((1, d)), _resident((d, 5 * d)), _resident((1, d)),
                  _resident((A_HEADS, CHUNK, CHUNK)), _resident((CHUNK, d)), _resident((d, d))],
        out_specs=[tile, tile, tile],
        out_shape=[jax.ShapeDtypeStruct((t, d), _BF16)] * 3,
        scratch_shapes=[pltpu.VMEM((MIX_TILE // MIX_IN_SUB, MIX_IN_SUB, d), dt)
                        for dt in (_F32, _BF16, _BF16)],
        compiler_params=_params(),
        name="mix_in",
    )(x, g, win, gv, ws, bs, wa)


def _fft_a_kernel(zb_ref, dft_ref, f_ref, tc_ref, ts_ref, br_ref, bi_ref, xr_scr, xi_scr):
    n1 = FFT_N1
    gd = B_GROUP_DIM
    dft = dft_ref[...].astype(_BF16)
    f_top, f_bot = f_ref[:n1, :], f_ref[n1:, :]
    z = jnp.swapaxes(zb_ref[...], 0, 1).reshape(FFT_S2_TILE * n1, D_MODEL)
    for grp in range(B_GROUPS):
        cols = slice(grp * gd, (grp + 1) * gd)
        spec = _dot(z[:, cols], dft)
        xr_scr[:, cols] = spec[:, :gd].astype(_BF16)
        xi_scr[:, cols] = spec[:, gd:].astype(_BF16)
    for t in range(FFT_S2_TILE):
        rows = slice(t * n1, (t + 1) * n1)
        tc = tc_ref[:, t:t + 1]
        ts = ts_ref[:, t:t + 1]
        m = jnp.concatenate([tc * f_top + ts * f_bot, tc * f_bot - ts * f_top], axis=0)
        b = _dot(m.astype(_BF16), jnp.concatenate([xr_scr[rows, :], xi_scr[rows, :]], axis=0))
        br_ref[t] = b[:n1].astype(_BF16)
        bi_ref[t] = b[n1:].astype(_BF16)


def _fft_a(zb, dft, fmat, tw_c, tw_s, batch, seq):
    d = D_MODEL
    n1, n2 = FFT_N1, seq // FFT_N1
    m = FFT_S2_TILE
    in_view = (batch, n1, n2, d)
    out_view = (batch, n2, n1, d)
    in_blk = pl.BlockSpec((None, n1, m, d), lambda b, j: (b, 0, j, 0))
    out_blk = pl.BlockSpec((None, m, n1, d), lambda b, j: (b, j, 0, 0))
    tw = pl.BlockSpec((None, n1, m), lambda b, j: (j, 0, 0))
    return pl.pallas_call(
        _fft_a_kernel,
        grid=(batch, n2 // m),
        in_specs=[in_blk, _resident((B_GROUP_DIM, 2 * B_GROUP_DIM)), _resident((2 * n1, 2 * n1)),
                  tw, tw],
        out_specs=[out_blk, out_blk],
        out_shape=[jax.ShapeDtypeStruct(out_view, _BF16)] * 2,
        scratch_shapes=[pltpu.VMEM((m * n1, d), _BF16), pltpu.VMEM((m * n1, d), _BF16)],
        compiler_params=_params(2),
        name="fft_a",
    )(zb.reshape(in_view), dft, fmat, tw_c, tw_s)


def _fft_b_kernel(br_ref, bi_ref, g_ref, f_ref, *, n2):
    gmat = g_ref[...].astype(_BF16)
    d = br_ref.shape[-1]
    n_slabs = br_ref.shape[1] * n2 // CHUNK
    br = jnp.swapaxes(br_ref[...], 0, 1).reshape(n_slabs, CHUNK, d)
    bi = jnp.swapaxes(bi_ref[...], 0, 1).reshape(n_slabs, CHUNK, d)
    ys = []
    for t in range(n_slabs):
        y = _dot(gmat, jnp.concatenate([br[t], bi[t]], axis=0)).astype(_BF16)
        ys.append(y.reshape(CHUNK // n2, n2, d))
    f_ref[...] = jnp.swapaxes(jnp.concatenate(ys, axis=0), 0, 1)


def _fft_b(br, bi, gmat, batch, seq):
    d = D_MODEL
    n1, n2 = FFT_N1, seq // FFT_N1
    k1s = FFT_B_ROWS // n2
    blk = pl.BlockSpec((None, n2, k1s, d), lambda b, j: (b, 0, j, 0))
    f = pl.pallas_call(
        functools.partial(_fft_b_kernel, n2=n2),
        grid=(batch, n1 // k1s),
        in_specs=[blk, blk, _resident((CHUNK, 2 * CHUNK))],
        out_specs=blk,
        out_shape=jax.ShapeDtypeStruct((batch, n2, n1, d), _BF16),
        compiler_params=_params(2),
        name="fft_b",
    )(br, bi, gmat)
    return f.reshape(batch * seq, d)


def _mix_out_ffn_kernel(x_ref, gaya_ref, gb_ref, f_ref, wb_ref, wo_ref, g_ref, wg_ref, wu_ref,
                        wd_ref, fg_ref, o_ref, a_ref, *, final):
    for s in range(MIX_TILE // MIX_SUB):
        rows = slice(s * MIX_SUB, (s + 1) * MIX_SUB)
        y_b = _dot(f_ref[rows, :], wb_ref[...])
        gated = gaya_ref[rows, :].astype(_F32) + gb_ref[rows, :].astype(_F32) * y_b
        x = x_ref[rows, :] + _dot(gated.astype(_BF16), wo_ref[...])
        out = _swiglu_residual(x, g_ref, wg_ref, wu_ref, wd_ref, a_ref)
        if final:
            out = _rms(out, fg_ref[...])
        o_ref[rows, :] = out


def _mix_out_ffn(x, gaya, gb, f, wb, wo, g, wg, wu, wd, fg, final):
    t = x.shape[0]
    d = D_MODEL
    tile = pl.BlockSpec((MIX_TILE, d), lambda i: (i, 0))
    return pl.pallas_call(
        functools.partial(_mix_out_ffn_kernel, final=final),
        grid=(t // MIX_TILE,),
        in_specs=[tile, tile, tile, tile, _resident((d, d)), _resident((d, d))] + _ffn_specs()
                 + [_resident((1, d))],
        out_specs=tile,
        out_shape=jax.ShapeDtypeStruct((t, d), _F32),
        scratch_shapes=[pltpu.VMEM((MIX_SUB, D_FF), _BF16)],
        compiler_params=_params(),
        name="mix_out_ffn",
    )(x, gaya, gb, f, wb, wo, g, wg, wu, wd, fg)


def _cos_sin(num, den):
    ang = 2.0 * np.pi * (np.asarray(num, dtype=np.int64) % den).astype(np.float64) / den
    return np.cos(ang), np.sin(ang)


def _channel_dft():
    n = B_GROUP_DIM
    c, s = _cos_sin(np.outer(np.arange(n), np.arange(n)), n)
    return jnp.asarray(np.concatenate([c, -s], axis=1) / np.sqrt(n), dtype=_F32)


def _position_dft(seq):
    n1, n2 = FFT_N1, seq // FFT_N1
    c1, s1 = _cos_sin(np.outer(np.arange(n1), np.arange(n1)), n1)
    fmat = np.block([[c1, s1], [-s1, c1]]) / np.sqrt(seq)
    tc, ts = _cos_sin(np.outer(np.arange(n1), np.arange(n2)), seq)
    m = FFT_S2_TILE
    tc = tc.reshape(n1, n2 // m, m).transpose(1, 0, 2)
    ts = ts.reshape(n1, n2 // m, m).transpose(1, 0, 2)
    c2, s2 = _cos_sin(np.outer(np.arange(n2), np.arange(n2)), n2)
    eye = np.eye(CHUNK // n2)
    gmat = np.concatenate([np.kron(eye, c2), np.kron(eye, s2)], axis=1)
    return tuple(jnp.asarray(t, dtype=_F32) for t in (fmat, tc, ts, gmat))


def kernel(x_prompt, x_sample, ffn1_norm, ffn1_w_gate, ffn1_w_up, ffn1_w_down, mix_norm, w_in,
           sgu_norm, sgu_w, sgu_b, w_branch_a, w_branch_b, w_out, ffn2_norm, ffn2_w_gate,
           ffn2_w_up, ffn2_w_down, final_norm):
    depth = w_in.shape[0]
    d = D_MODEL
    bf = lambda w: w.astype(_BF16)
    row = lambda v: v.reshape(1, d).astype(_F32)
    dft = _channel_dft()
    bias = jnp.repeat(jnp.swapaxes(sgu_b, 1, 2), A_HEAD_DIM, axis=2).astype(_F32)
    weights = dict(
        wg1=bf(ffn1_w_gate), wu1=bf(ffn1_w_up), wd1=bf(ffn1_w_down), win=bf(w_in), ws=bf(sgu_w),
        wa=bf(w_branch_a), wb=bf(w_branch_b), wo=bf(w_out), wg2=bf(ffn2_w_gate),
        wu2=bf(ffn2_w_up), wd2=bf(ffn2_w_down))
    fg = row(final_norm)

    def trunk(x3):
        batch, seq, _ = x3.shape
        fmat, tw_c, tw_s, gmat = _position_dft(seq)
        x = x3.reshape(batch * seq, d)
        for l in range(depth):
            w = {k: v[l] for k, v in weights.items()}
            x = _ffn(x, row(ffn1_norm[l]), w["wg1"], w["wu1"], w["wd1"])
            gaya, gb, zb = _mix_in(x, row(mix_norm[l]), w["win"], row(sgu_norm[l]), w["ws"],
                                   bias[l], w["wa"])
            br, bi = _fft_a(zb, dft, fmat, tw_c, tw_s, batch, seq)
            f = _fft_b(br, bi, gmat, batch, seq)
            x = _mix_out_ffn(x, gaya, gb, f, w["wb"], w["wo"], row(ffn2_norm[l]), w["wg2"],
                             w["wu2"], w["wd2"], fg, final=(l == depth - 1))
        return x.reshape(batch, seq, d)

    return (trunk(x_prompt), trunk(x_sample))
```

```python
import functools

import numpy as np
import jax
import jax.numpy as jnp
from jax import lax
from jax.experimental import pallas as pl
from jax.experimental.pallas import tpu as pltpu

D_MODEL = 1024
CHUNK = 128
A_HEADS = 8
A_HEAD_DIM = D_MODEL // A_HEADS
B_GROUPS = 4
B_GROUP_DIM = D_MODEL // B_GROUPS
D_FF = 2816
EPS = 1e-6

FFN_TILE = 1024
MIX_TILE = 1024
MIX_IN_SUB = 256
MIX_SUB = 512
FF_COLS = 256
FFT_N1 = 128
FFT_S2_TILE = 16
FFT_B_ROWS = 2048
FFT_B_GROUP = 4
VMEM_LIMIT_BYTES = 56 * 1024 * 1024

_BF16 = jnp.bfloat16
_F32 = jnp.float32


def _dot(a, b):
    return jnp.dot(a, b, preferred_element_type=_F32)


def _rms(x, g):
    return x * lax.rsqrt(jnp.mean(x * x, axis=-1, keepdims=True) + EPS) * g


def _gelu(x):
    return 0.5 * x * (1.0 + lax.erf(x * np.float32(np.sqrt(0.5))))


def _resident(shape):
    return pl.BlockSpec(shape, lambda *_: (0,) * len(shape), pipeline_mode=pl.Buffered(1))


def _params(n_grid_dims=1):
    return pltpu.CompilerParams(dimension_semantics=("arbitrary",) * n_grid_dims,
                                vmem_limit_bytes=VMEM_LIMIT_BYTES)


def _swiglu_residual(x, g_ref, wg_ref, wu_ref, wd_ref, a_ref):
    h = _rms(x, g_ref[...]).astype(_BF16)
    for c in range(0, D_FF, FF_COLS):
        gate = _dot(h, wg_ref[:, c:c + FF_COLS])
        up = _dot(h, wu_ref[:, c:c + FF_COLS])
        a_ref[:, c:c + FF_COLS] = (jax.nn.silu(gate) * up).astype(_BF16)
    return x + 0.5 * _dot(a_ref[...], wd_ref[...])


def _ffn_kernel(x_ref, g_ref, wg_ref, wu_ref, wd_ref, o_ref, a_ref):
    o_ref[...] = _swiglu_residual(x_ref[...], g_ref, wg_ref, wu_ref, wd_ref, a_ref)


def _ffn_specs():
    return [_resident((1, D_MODEL)), _resident((D_MODEL, D_FF)), _resident((D_MODEL, D_FF)),
            _resident((D_FF, D_MODEL))]


def _ffn(x, g, wg, wu, wd):
    t = x.shape[0]
    tile = pl.BlockSpec((FFN_TILE, D_MODEL), lambda i: (i, 0))
    return pl.pallas_call(
        _ffn_kernel,
        grid=(t // FFN_TILE,),
        in_specs=[tile] + _ffn_specs(),
        out_specs=tile,
        out_shape=jax.ShapeDtypeStruct((t, D_MODEL), _F32),
        scratch_shapes=[pltpu.VMEM((FFN_TILE, D_FF), _BF16)],
        compiler_params=_params(),
        name="ffn",
    )(x, g, wg, wu, wd)


def _mix_in_kernel(x_ref, g_ref, win_ref, gv_ref, ws_ref, bs_ref, wa_ref,
                   gaya_ref, gb_ref, zb_ref, u_scr, v_scr, sg_scr):
    d = D_MODEL
    n_chunks = MIX_IN_SUB // CHUNK
    n_pass = MIX_TILE // MIX_IN_SUB
    rows_of = [slice(s * MIX_IN_SUB, (s + 1) * MIX_IN_SUB) for s in range(n_pass)]

    def front(s):
        h = _rms(x_ref[rows_of[s], :], g_ref[...]).astype(_BF16)
        u_scr[s] = _gelu(_dot(h, win_ref[:, 0:d]))
        return h, _gelu(_dot(h, win_ref[:, d:2 * d]))

    def gate_b(s, h):
        gb_ref[rows_of[s], :] = jax.nn.sigmoid(_dot(h, win_ref[:, 4 * d:5 * d])).astype(_BF16)

    def spatial_gate(s):
        for hd in range(A_HEADS):
            lanes = slice(hd * A_HEAD_DIM, (hd + 1) * A_HEAD_DIM)
            rhs = jnp.concatenate(
                [v_scr[s, c * CHUNK:(c + 1) * CHUNK, lanes] for c in range(n_chunks)], axis=1)
            mixed = _dot(ws_ref[hd], rhs)
            for c in range(n_chunks):
                rows = slice(c * CHUNK, (c + 1) * CHUNK)
                sp = mixed[:, c * A_HEAD_DIM:(c + 1) * A_HEAD_DIM] + bs_ref[:, lanes]
                sg_scr[s, rows, lanes] = (u_scr[s, rows, lanes] * sp).astype(_BF16)

    def branch_a(s, h):
        y_a = _dot(sg_scr[s], wa_ref[...])
        gate_a = jax.nn.sigmoid(_dot(h, win_ref[:, 3 * d:4 * d]))
        gaya_ref[rows_of[s], :] = (gate_a * y_a).astype(_BF16)

    def fourier(s, h):
        zb_ref[rows_of[s], :] = _dot(h, win_ref[:, 2 * d:3 * d]).astype(_BF16)

    h, v = front(0)
    for s in range(n_pass):
        gate_b(s, h)
        v_scr[s] = _rms(v, gv_ref[...]).astype(_BF16)
        if s + 1 < n_pass:
            h_next, v = front(s + 1)
        spatial_gate(s)
        branch_a(s, h)
        fourier(s, h)
        if s + 1 < n_pass:
            h = h_next


def _mix_in(x, g, win, gv, ws, bs, wa):
    t = x.shape[0]
    d = D_MODEL
    tile = pl.BlockSpec((MIX_TILE, d), lambda i: (i, 0))
    return pl.pallas_call(
        _mix_in_kernel,
        grid=(t // MIX_TILE,),
        in_specs=[tile, _resident((1, d)), _resident((d, 5 * d)), _resident((1, d)),
                  _resident((A_HEADS, CHUNK, CHUNK)), _resident((CHUNK, d)), _resident((d, d))],
        out_specs=[tile, tile, tile],
        out_shape=[jax.ShapeDtypeStruct((t, d), _BF16)] * 3,
        scratch_shapes=[pltpu.VMEM((MIX_TILE // MIX_IN_SUB, MIX_IN_SUB, d), dt)
                        for dt in (_F32, _BF16, _BF16)],
        compiler_params=_params(),
        name="mix_in",
    )(x, g, win, gv, ws, bs, wa)


def _fft_a_kernel(zb_ref, dft_ref, f_ref, tc_ref, ts_ref, br_ref, bi_ref, xr_scr, xi_scr):
    n1 = FFT_N1
    gd = B_GROUP_DIM
    dft = dft_ref[...].astype(_BF16)
    f_top, f_bot = f_ref[:n1, :], f_ref[n1:, :]
    z = jnp.swapaxes(zb_ref[...], 0, 1).reshape(FFT_S2_TILE * n1, D_MODEL)
    for grp in range(B_GROUPS):
        cols = slice(grp * gd, (grp + 1) * gd)
        spec = _dot(z[:, cols], dft)
        xr_scr[:, cols] = spec[:, :gd].astype(_BF16)
        xi_scr[:, cols] = spec[:, gd:].astype(_BF16)
    for t in range(FFT_S2_TILE):
        rows = slice(t * n1, (t + 1) * n1)
        tc = tc_ref[:, t:t + 1]
        ts = ts_ref[:, t:t + 1]
        m = jnp.concatenate([tc * f_top + ts * f_bot, tc * f_bot - ts * f_top], axis=0)
        b = _dot(m.astype(_BF16), jnp.concatenate([xr_scr[rows, :], xi_scr[rows, :]], axis=0))
        br_ref[t] = b[:n1].astype(_BF16)
        bi_ref[t] = b[n1:].astype(_BF16)


def _fft_a(zb, dft, fmat, tw_c, tw_s, batch, seq):
    d = D_MODEL
    n1, n2 = FFT_N1, seq // FFT_N1
    m = FFT_S2_TILE
    in_view = (batch, n1, n2, d)
    out_view = (batch, n2, n1, d)
    in_blk = pl.BlockSpec((None, n1, m, d), lambda b, j: (b, 0, j, 0))
    out_blk = pl.BlockSpec((None, m, n1, d), lambda b, j: (b, j, 0, 0))
    tw = pl.BlockSpec((None, n1, m), lambda b, j: (j, 0, 0))
    return pl.pallas_call(
        _fft_a_kernel,
        grid=(batch, n2 // m),
        in_specs=[in_blk, _resident((B_GROUP_DIM, 2 * B_GROUP_DIM)), _resident((2 * n1, 2 * n1)),
                  tw, tw],
        out_specs=[out_blk, out_blk],
        out_shape=[jax.ShapeDtypeStruct(out_view, _BF16)] * 2,
        scratch_shapes=[pltpu.VMEM((m * n1, d), _BF16), pltpu.VMEM((m * n1, d), _BF16)],
        compiler_params=_params(2),
        name="fft_a",
    )(zb.reshape(in_view), dft, fmat, tw_c, tw_s)


def _fft_b_kernel(br_ref, bi_ref, g_ref, wb_ref, yb_ref, *, n2):
    gmat = g_ref[...].astype(_BF16)
    d = br_ref.shape[-1]
    n_slabs = br_ref.shape[1] * n2 // CHUNK
    br = jnp.swapaxes(br_ref[...], 0, 1).reshape(n_slabs, CHUNK, d)
    bi = jnp.swapaxes(bi_ref[...], 0, 1).reshape(n_slabs, CHUNK, d)
    outs = []
    for first in range(0, n_slabs, FFT_B_GROUP):
        fourier = [_dot(gmat, jnp.concatenate([br[t], bi[t]], axis=0)).astype(_BF16)
                   for t in range(first, first + FFT_B_GROUP)]
        y_b = _dot(jnp.concatenate(fourier, axis=0), wb_ref[...]).astype(_BF16)
        outs.append(y_b.reshape(FFT_B_GROUP * CHUNK // n2, n2, d))
    yb_ref[...] = jnp.swapaxes(jnp.concatenate(outs, axis=0), 0, 1)


def _fft_b(br, bi, gmat, wb, batch, seq):
    d = D_MODEL
    n1, n2 = FFT_N1, seq // FFT_N1
    k1s = FFT_B_ROWS // n2
    blk = pl.BlockSpec((None, n2, k1s, d), lambda b, j: (b, 0, j, 0))
    y_b = pl.pallas_call(
        functools.partial(_fft_b_kernel, n2=n2),
        grid=(batch, n1 // k1s),
        in_specs=[blk, blk, _resident((CHUNK, 2 * CHUNK)), _resident((d, d))],
        out_specs=blk,
        out_shape=jax.ShapeDtypeStruct((batch, n2, n1, d), _BF16),
        compiler_params=_params(2),
        name="fft_b",
    )(br, bi, gmat, wb)
    return y_b.reshape(batch * seq, d)


def _mix_out_ffn_kernel(x_ref, gaya_ref, gb_ref, yb_ref, wo_ref, g_ref, wg_ref, wu_ref,
                        wd_ref, fg_ref, o_ref, a_ref, *, final):
    for s in range(MIX_TILE // MIX_SUB):
        rows = slice(s * MIX_SUB, (s + 1) * MIX_SUB)
        gated = (gaya_ref[rows, :].astype(_F32)
                 + gb_ref[rows, :].astype(_F32) * yb_ref[rows, :].astype(_F32))
        x = x_ref[rows, :] + _dot(gated.astype(_BF16), wo_ref[...])
        out = _swiglu_residual(x, g_ref, wg_ref, wu_ref, wd_ref, a_ref)
        if final:
            out = _rms(out, fg_ref[...])
        o_ref[rows, :] = out


def _mix_out_ffn(x, gaya, gb, y_b, wo, g, wg, wu, wd, fg, final):
    t = x.shape[0]
    d = D_MODEL
    tile = pl.BlockSpec((MIX_TILE, d), lambda i: (i, 0))
    return pl.pallas_call(
        functools.partial(_mix_out_ffn_kernel, final=final),
        grid=(t // MIX_TILE,),
        in_specs=[tile, tile, tile, tile, _resident((d, d))] + _ffn_specs() + [_resident((1, d))],
        out_specs=tile,
        out_shape=jax.ShapeDtypeStruct((t, d), _F32),
        scratch_shapes=[pltpu.VMEM((MIX_SUB, D_FF), _BF16)],
        compiler_params=_params(),
        name="mix_out_ffn",
    )(x, gaya, gb, y_b, wo, g, wg, wu, wd, fg)


def _cos_sin(num, den):
    ang = 2.0 * np.pi * (np.asarray(num, dtype=np.int64) % den).astype(np.float64) / den
    return np.cos(ang), np.sin(ang)


def _channel_dft():
    n = B_GROUP_DIM
    c, s = _cos_sin(np.outer(np.arange(n), np.arange(n)), n)
    return jnp.asarray(np.concatenate([c, -s], axis=1) / np.sqrt(n), dtype=_F32)


def _position_dft(seq):
    n1, n2 = FFT_N1, seq // FFT_N1
    c1, s1 = _cos_sin(np.outer(np.arange(n1), np.arange(n1)), n1)
    fmat = np.block([[c1, s1], [-s1, c1]]) / np.sqrt(seq)
    tc, ts = _cos_sin(np.outer(np.arange(n1), np.arange(n2)), seq)
    m = FFT_S2_TILE
    tc = tc.reshape(n1, n2 // m, m).transpose(1, 0, 2)
    ts = ts.reshape(n1, n2 // m, m).transpose(1, 0, 2)
    c2, s2 = _cos_sin(np.outer(np.arange(n2), np.arange(n2)), n2)
    eye = np.eye(CHUNK // n2)
    gmat = np.concatenate([np.kron(eye, c2), np.kron(eye, s2)], axis=1)
    return tuple(jnp.asarray(t, dtype=_F32) for t in (fmat, tc, ts, gmat))


def kernel(x_prompt, x_sample, ffn1_norm, ffn1_w_gate, ffn1_w_up, ffn1_w_down, mix_norm, w_in,
           sgu_norm, sgu_w, sgu_b, w_branch_a, w_branch_b, w_out, ffn2_norm, ffn2_w_gate,
           ffn2_w_up, ffn2_w_down, final_norm):
    depth = w_in.shape[0]
    d = D_MODEL
    bf = lambda w: w.astype(_BF16)
    row = lambda v: v.reshape(1, d).astype(_F32)
    dft = _channel_dft()
    bias = jnp.repeat(jnp.swapaxes(sgu_b, 1, 2), A_HEAD_DIM, axis=2).astype(_F32)
    weights = dict(
        wg1=bf(ffn1_w_gate), wu1=bf(ffn1_w_up), wd1=bf(ffn1_w_down), win=bf(w_in), ws=bf(sgu_w),
        wa=bf(w_branch_a), wb=bf(w_branch_b), wo=bf(w_out), wg2=bf(ffn2_w_gate),
        wu2=bf(ffn2_w_up), wd2=bf(ffn2_w_down))
    fg = row(final_norm)

    def trunk(x3):
        batch, seq, _ = x3.shape
        fmat, tw_c, tw_s, gmat = _position_dft(seq)
        x = x3.reshape(batch * seq, d)
        for l in range(depth):
            w = {k: v[l] for k, v in weights.items()}
            x = _ffn(x, row(ffn1_norm[l]), w["wg1"], w["wu1"], w["wd1"])
            gaya, gb, zb = _mix_in(x, row(mix_norm[l]), w["win"], row(sgu_norm[l]), w["ws"],
                                   bias[l], w["wa"])
            br, bi = _fft_a(zb, dft, fmat, tw_c, tw_s, batch, seq)
            y_b = _fft_b(br, bi, gmat, w["wb"], batch, seq)
            x = _mix_out_ffn(x, gaya, gb, y_b, w["wo"], row(ffn2_norm[l]), w["wg2"],
                             w["wu2"], w["wd2"], fg, final=(l == depth - 1))
        return x.reshape(batch, seq, d)

    return (trunk(x_prompt), trunk(x_sample))
```

```python
import functools

import numpy as np
import jax
import jax.numpy as jnp
from jax import lax
from jax.experimental import pallas as pl
from jax.experimental.pallas import tpu as pltpu

D_MODEL = 1024
CHUNK = 128
A_HEADS = 8
A_HEAD_DIM = D_MODEL // A_HEADS
B_GROUPS = 4
B_GROUP_DIM = D_MODEL // B_GROUPS
D_FF = 2816
EPS = 1e-6

FFN_TILE = 1024
MIX_TILE = 1024
MIX_IN_SUB = 256
MIX_SUB = 512
FF_COLS = 256
FFT_N1 = 128
FFT_S2_TILE = 16
FFT_B_ROWS = 2048
FFT_B_GROUP = 4
V7X_VMEM_BYTES = 64 * 1024 * 1024
VMEM_LIMIT_BYTES = V7X_VMEM_BYTES * 7 // 8

_BF16 = jnp.bfloat16
_F32 = jnp.float32


def _dot(a, b):
    return jnp.dot(a, b, preferred_element_type=_F32)


def _rms(x, g):
    return x * lax.rsqrt(jnp.mean(x * x, axis=-1, keepdims=True) + EPS) * g


def _gelu(x):
    return 0.5 * x * (1.0 + lax.erf(x * np.float32(np.sqrt(0.5))))


def _resident(shape):
    return pl.BlockSpec(shape, lambda *_: (0,) * len(shape), pipeline_mode=pl.Buffered(1))


def _params(n_grid_dims=1):
    return pltpu.CompilerParams(dimension_semantics=("arbitrary",) * n_grid_dims,
                                vmem_limit_bytes=VMEM_LIMIT_BYTES)


def _swiglu_residual(x, g_ref, wg_ref, wu_ref, wd_ref, a_ref):
    h = _rms(x, g_ref[...]).astype(_BF16)
    for c in range(0, D_FF, FF_COLS):
        gate = _dot(h, wg_ref[:, c:c + FF_COLS])
        up = _dot(h, wu_ref[:, c:c + FF_COLS])
        a_ref[:, c:c + FF_COLS] = (jax.nn.silu(gate) * up).astype(_BF16)
    return x + 0.5 * _dot(a_ref[...], wd_ref[...])


def _ffn_kernel(x_ref, g_ref, wg_ref, wu_ref, wd_ref, o_ref, a_ref):
    o_ref[...] = _swiglu_residual(x_ref[...], g_ref, wg_ref, wu_ref, wd_ref, a_ref)


def _ffn_specs():
    return [_resident((1, D_MODEL)), _resident((D_MODEL, D_FF)), _resident((D_MODEL, D_FF)),
            _resident((D_FF, D_MODEL))]


def _ffn(x, g, wg, wu, wd):
    t = x.shape[0]
    tile = pl.BlockSpec((FFN_TILE, D_MODEL), lambda i: (i, 0))
    return pl.pallas_call(
        _ffn_kernel,
        grid=(t // FFN_TILE,),
        in_specs=[tile] + _ffn_specs(),
        out_specs=tile,
        out_shape=jax.ShapeDtypeStruct((t, D_MODEL), _F32),
        scratch_shapes=[pltpu.VMEM((FFN_TILE, D_FF), _BF16)],
        compiler_params=_params(),
        name="ffn",
    )(x, g, wg, wu, wd)


def _mix_in_kernel(x_ref, g_ref, win_ref, gv_ref, ws_ref, bs_ref, wa_ref,
                   gaya_ref, gb_ref, zb_ref, u_scr, v_scr, sg_scr):
    d = D_MODEL
    n_chunks = MIX_IN_SUB // CHUNK
    n_pass = MIX_TILE // MIX_IN_SUB
    rows_of = [slice(s * MIX_IN_SUB, (s + 1) * MIX_IN_SUB) for s in range(n_pass)]

    def front(s):
        h = _rms(x_ref[rows_of[s], :], g_ref[...]).astype(_BF16)
        u_scr[s] = _gelu(_dot(h, win_ref[:, 0:d]))
        return h, _gelu(_dot(h, win_ref[:, d:2 * d]))

    def gate_b(s, h):
        gb_ref[rows_of[s], :] = jax.nn.sigmoid(_dot(h, win_ref[:, 4 * d:5 * d])).astype(_BF16)

    def spatial_gate(s):
        for hd in range(A_HEADS):
            lanes = slice(hd * A_HEAD_DIM, (hd + 1) * A_HEAD_DIM)
            rhs = jnp.concatenate(
                [v_scr[s, c * CHUNK:(c + 1) * CHUNK, lanes] for c in range(n_chunks)], axis=1)
            mixed = _dot(ws_ref[hd], rhs)
            for c in range(n_chunks):
                rows = slice(c * CHUNK, (c + 1) * CHUNK)
                sp = mixed[:, c * A_HEAD_DIM:(c + 1) * A_HEAD_DIM] + bs_ref[:, lanes]
                sg_scr[s, rows, lanes] = (u_scr[s, rows, lanes] * sp).astype(_BF16)

    def branch_a(s, h):
        y_a = _dot(sg_scr[s], wa_ref[...])
        gate_a = jax.nn.sigmoid(_dot(h, win_ref[:, 3 * d:4 * d]))
        gaya_ref[rows_of[s], :] = (gate_a * y_a).astype(_BF16)

    def fourier(s, h):
        zb_ref[rows_of[s], :] = _dot(h, win_ref[:, 2 * d:3 * d]).astype(_BF16)

    h, v = front(0)
    for s in range(n_pass):
        gate_b(s, h)
        v_scr[s] = _rms(v, gv_ref[...]).astype(_BF16)
        if s + 1 < n_pass:
            h_next, v = front(s + 1)
        spatial_gate(s)
        branch_a(s, h)
        fourier(s, h)
        if s + 1 < n_pass:
            h = h_next


def _mix_in(x, g, win, gv, ws, bs, wa):
    t = x.shape[0]
    d = D_MODEL
    tile = pl.BlockSpec((MIX_TILE, d), lambda i: (i, 0))
    return pl.pallas_call(
        _mix_in_kernel,
        grid=(t // MIX_TILE,),
        in_specs=[tile, _resident((1, d)), _resident((d, 5 * d)), _resident((1, d)),
                  _resident((A_HEADS, CHUNK, CHUNK)), _resident((CHUNK, d)), _resident((d, d))],
        out_specs=[tile, tile, tile],
        out_shape=[jax.ShapeDtypeStruct((t, d), _BF16)] * 3,
        scratch_shapes=[pltpu.VMEM((MIX_TILE // MIX_IN_SUB, MIX_IN_SUB, d), dt)
                        for dt in (_F32, _BF16, _BF16)],
        compiler_params=_params(),
        name="mix_in",
    )(x, g, win, gv, ws, bs, wa)


def _fft_a_kernel(zb_ref, dft_ref, f_ref, tc_ref, ts_ref, br_ref, bi_ref, xr_scr, xi_scr):
    n1 = FFT_N1
    gd = B_GROUP_DIM
    dft = dft_ref[...].astype(_BF16)
    f_top, f_bot = f_ref[:n1, :], f_ref[n1:, :]
    z = jnp.swapaxes(zb_ref[...], 0, 1).reshape(FFT_S2_TILE * n1, D_MODEL)
    for grp in range(B_GROUPS):
        cols = slice(grp * gd, (grp + 1) * gd)
        spec = _dot(z[:, cols], dft)
        xr_scr[:, cols] = spec[:, :gd].astype(_BF16)
        xi_scr[:, cols] = spec[:, gd:].astype(_BF16)
    for t in range(FFT_S2_TILE):
        rows = slice(t * n1, (t + 1) * n1)
        tc = tc_ref[:, t:t + 1]
        ts = ts_ref[:, t:t + 1]
        m = jnp.concatenate([tc * f_top + ts * f_bot, tc * f_bot - ts * f_top], axis=0)
        b = _dot(m.astype(_BF16), jnp.concatenate([xr_scr[rows, :], xi_scr[rows, :]], axis=0))
        br_ref[t] = b[:n1].astype(_BF16)
        bi_ref[t] = b[n1:].astype(_BF16)


def _fft_a(zb, dft, fmat, tw_c, tw_s, batch, seq):
    d = D_MODEL
    n1, n2 = FFT_N1, seq // FFT_N1
    m = FFT_S2_TILE
    in_view = (batch, n1, n2, d)
    out_view = (batch, n2, n1, d)
    in_blk = pl.BlockSpec((None, n1, m, d), lambda b, j: (b, 0, j, 0))
    out_blk = pl.BlockSpec((None, m, n1, d), lambda b, j: (b, j, 0, 0))
    tw = pl.BlockSpec((None, n1, m), lambda b, j: (j, 0, 0))
    return pl.pallas_call(
        _fft_a_kernel,
        grid=(batch, n2 // m),
        in_specs=[in_blk, _resident((B_GROUP_DIM, 2 * B_GROUP_DIM)), _resident((2 * n1, 2 * n1)),
                  tw, tw],
        out_specs=[out_blk, out_blk],
        out_shape=[jax.ShapeDtypeStruct(out_view, _BF16)] * 2,
        scratch_shapes=[pltpu.VMEM((m * n1, d), _BF16), pltpu.VMEM((m * n1, d), _BF16)],
        compiler_params=_params(2),
        name="fft_a",
    )(zb.reshape(in_view), dft, fmat, tw_c, tw_s)


def _fft_b_kernel(br_ref, bi_ref, g_ref, wb_ref, yb_ref, *, n2):
    gmat = g_ref[...].astype(_BF16)
    d = br_ref.shape[-1]
    n_slabs = br_ref.shape[1] * n2 // CHUNK
    br = jnp.swapaxes(br_ref[...], 0, 1).reshape(n_slabs, CHUNK, d)
    bi = jnp.swapaxes(bi_ref[...], 0, 1).reshape(n_slabs, CHUNK, d)
    outs = []
    for first in range(0, n_slabs, FFT_B_GROUP):
        fourier = [_dot(gmat, jnp.concatenate([br[t], bi[t]], axis=0)).astype(_BF16)
                   for t in range(first, first + FFT_B_GROUP)]
        y_b = _dot(jnp.concatenate(fourier, axis=0), wb_ref[...]).astype(_BF16)
        outs.append(y_b.reshape(FFT_B_GROUP * CHUNK // n2, n2, d))
    yb_ref[...] = jnp.swapaxes(jnp.concatenate(outs, axis=0), 0, 1)


def _fft_b(br, bi, gmat, wb, batch, seq):
    d = D_MODEL
    n1, n2 = FFT_N1, seq // FFT_N1
    k1s = FFT_B_ROWS // n2
    blk = pl.BlockSpec((None, n2, k1s, d), lambda b, j: (b, 0, j, 0))
    y_b = pl.pallas_call(
        functools.partial(_fft_b_kernel, n2=n2),
        grid=(batch, n1 // k1s),
        in_specs=[blk, blk, _resident((CHUNK, 2 * CHUNK)), _resident((d, d))],
        out_specs=blk,
        out_shape=jax.ShapeDtypeStruct((batch, n2, n1, d), _BF16),
        compiler_params=_params(2),
        name="fft_b",
    )(br, bi, gmat, wb)
    return y_b.reshape(batch * seq, d)


def _mix_out_ffn_kernel(x_ref, gaya_ref, gb_ref, yb_ref, wo_ref, g_ref, wg_ref, wu_ref,
                        wd_ref, fg_ref, o_ref, a_ref, *, final):
    for s in range(MIX_TILE // MIX_SUB):
        rows = slice(s * MIX_SUB, (s + 1) * MIX_SUB)
        gated = (gaya_ref[rows, :].astype(_F32)
                 + gb_ref[rows, :].astype(_F32) * yb_ref[rows, :].astype(_F32))
        x = x_ref[rows, :] + _dot(gated.astype(_BF16), wo_ref[...])
        out = _swiglu_residual(x, g_ref, wg_ref, wu_ref, wd_ref, a_ref)
        if final:
            out = _rms(out, fg_ref[...])
        o_ref[rows, :] = out


def _mix_out_ffn(x, gaya, gb, y_b, wo, g, wg, wu, wd, fg, final):
    t = x.shape[0]
    d = D_MODEL
    tile = pl.BlockSpec((MIX_TILE, d), lambda i: (i, 0))
    return pl.pallas_call(
        functools.partial(_mix_out_ffn_kernel, final=final),
        grid=(t // MIX_TILE,),
        in_specs=[tile, tile, tile, tile, _resident((d, d))] + _ffn_specs() + [_resident((1, d))],
        out_specs=tile,
        out_shape=jax.ShapeDtypeStruct((t, d), _F32),
        scratch_shapes=[pltpu.VMEM((MIX_SUB, D_FF), _BF16)],
        compiler_params=_params(),
        name="mix_out_ffn",
    )(x, gaya, gb, y_b, wo, g, wg, wu, wd, fg)


def _cos_sin(num, den):
    ang = 2.0 * np.pi * (np.asarray(num, dtype=np.int64) % den).astype(np.float64) / den
    return np.cos(ang), np.sin(ang)


def _channel_dft():
    n = B_GROUP_DIM
    c, s = _cos_sin(np.outer(np.arange(n), np.arange(n)), n)
    return jnp.asarray(np.concatenate([c, -s], axis=1) / np.sqrt(n), dtype=_F32)


def _position_dft(seq):
    n1, n2 = FFT_N1, seq // FFT_N1
    c1, s1 = _cos_sin(np.outer(np.arange(n1), np.arange(n1)), n1)
    fmat = np.block([[c1, s1], [-s1, c1]]) / np.sqrt(seq)
    tc, ts = _cos_sin(np.outer(np.arange(n1), np.arange(n2)), seq)
    m = FFT_S2_TILE
    tc = tc.reshape(n1, n2 // m, m).transpose(1, 0, 2)
    ts = ts.reshape(n1, n2 // m, m).transpose(1, 0, 2)
    c2, s2 = _cos_sin(np.outer(np.arange(n2), np.arange(n2)), n2)
    eye = np.eye(CHUNK // n2)
    gmat = np.concatenate([np.kron(eye, c2), np.kron(eye, s2)], axis=1)
    return tuple(jnp.asarray(t, dtype=_F32) for t in (fmat, tc, ts, gmat))


def kernel(x_prompt, x_sample, ffn1_norm, ffn1_w_gate, ffn1_w_up, ffn1_w_down, mix_norm, w_in,
           sgu_norm, sgu_w, sgu_b, w_branch_a, w_branch_b, w_out, ffn2_norm, ffn2_w_gate,
           ffn2_w_up, ffn2_w_down, final_norm):
    depth = w_in.shape[0]
    d = D_MODEL
    bf = lambda w: w.astype(_BF16)
    row = lambda v: v.reshape(1, d).astype(_F32)
    dft = _channel_dft()
    bias = jnp.repeat(jnp.swapaxes(sgu_b, 1, 2), A_HEAD_DIM, axis=2).astype(_F32)
    weights = dict(
        wg1=bf(ffn1_w_gate), wu1=bf(ffn1_w_up), wd1=bf(ffn1_w_down), win=bf(w_in), ws=bf(sgu_w),
        wa=bf(w_branch_a), wb=bf(w_branch_b), wo=bf(w_out), wg2=bf(ffn2_w_gate),
        wu2=bf(ffn2_w_up), wd2=bf(ffn2_w_down))
    fg = row(final_norm)

    def trunk(x3):
        batch, seq, _ = x3.shape
        fmat, tw_c, tw_s, gmat = _position_dft(seq)
        x = x3.reshape(batch * seq, d)
        for l in range(depth):
            w = {k: v[l] for k, v in weights.items()}
            x = _ffn(x, row(ffn1_norm[l]), w["wg1"], w["wu1"], w["wd1"])
            gaya, gb, zb = _mix_in(x, row(mix_norm[l]), w["win"], row(sgu_norm[l]), w["ws"],
                                   bias[l], w["wa"])
            br, bi = _fft_a(zb, dft, fmat, tw_c, tw_s, batch, seq)
            y_b = _fft_b(br, bi, gmat, w["wb"], batch, seq)
            x = _mix_out_ffn(x, gaya, gb, y_b, w["wo"], row(ffn2_norm[l]), w["wg2"],
                             w["wu2"], w["wd2"], fg, final=(l == depth - 1))
        return x.reshape(batch, seq, d)

    return (trunk(x_prompt), trunk(x_sample))
```

```python
import functools

import numpy as np
import jax
import jax.numpy as jnp
from jax import lax
from jax.experimental import pallas as pl
from jax.experimental.pallas import tpu as pltpu

D_MODEL = 1024
CHUNK = 128
A_HEADS = 8
A_HEAD_DIM = D_MODEL // A_HEADS
B_GROUPS = 4
B_GROUP_DIM = D_MODEL // B_GROUPS
D_FF = 2816
EPS = 1e-6

FFN_TILE = 1024
MIX_TILE = 1024
MIX_IN_SUB = 256
MIX_SUB = 512
FF_COLS = 256
FFT_N1 = 128
FFT_S2_TILE = 16
FFT_B_ROWS = 2048
FFT_B_GROUP = 4
V7X_VMEM_BYTES = 64 * 1024 * 1024
VMEM_LIMIT_BYTES = V7X_VMEM_BYTES * 7 // 8

_BF16 = jnp.bfloat16
_F32 = jnp.float32


def _dot(a, b):
    return jnp.dot(a, b, preferred_element_type=_F32)


def _rms(x, g):
    return x * lax.rsqrt(jnp.mean(x * x, axis=-1, keepdims=True) + EPS) * g


def _gelu(x):
    return 0.5 * x * (1.0 + lax.erf(x * np.float32(np.sqrt(0.5))))


def _resident(shape):
    return pl.BlockSpec(shape, lambda *_: (0,) * len(shape), pipeline_mode=pl.Buffered(1))


def _params(n_grid_dims=1):
    return pltpu.CompilerParams(dimension_semantics=("arbitrary",) * n_grid_dims,
                                vmem_limit_bytes=VMEM_LIMIT_BYTES)


def _swiglu_hidden(h, wg_ref, wu_ref, a_ref, first_col, last_col):
    for c in range(first_col, last_col, FF_COLS):
        gate = _dot(h, wg_ref[:, c:c + FF_COLS])
        up = _dot(h, wu_ref[:, c:c + FF_COLS])
        a_ref[:, c:c + FF_COLS] = (jax.nn.silu(gate) * up).astype(_BF16)


def _swiglu_residual(x, g_ref, wg_ref, wu_ref, wd_ref, a_ref):
    h = _rms(x, g_ref[...]).astype(_BF16)
    _swiglu_hidden(h, wg_ref, wu_ref, a_ref, 0, D_FF)
    return x + 0.5 * _dot(a_ref[...], wd_ref[...])


def _ffn_kernel(x_ref, g_ref, wg_ref, wu_ref, wd_ref, o_ref, a_ref):
    o_ref[...] = _swiglu_residual(x_ref[...], g_ref, wg_ref, wu_ref, wd_ref, a_ref)


def _ffn_specs():
    return [_resident((1, D_MODEL)), _resident((D_MODEL, D_FF)), _resident((D_MODEL, D_FF)),
            _resident((D_FF, D_MODEL))]


def _ffn(x, g, wg, wu, wd):
    t = x.shape[0]
    tile = pl.BlockSpec((FFN_TILE, D_MODEL), lambda i: (i, 0))
    return pl.pallas_call(
        _ffn_kernel,
        grid=(t // FFN_TILE,),
        in_specs=[tile] + _ffn_specs(),
        out_specs=tile,
        out_shape=jax.ShapeDtypeStruct((t, D_MODEL), _F32),
        scratch_shapes=[pltpu.VMEM((FFN_TILE, D_FF), _BF16)],
        compiler_params=_params(),
        name="ffn",
    )(x, g, wg, wu, wd)


def _mix_in_kernel(x_ref, g_ref, win_ref, gv_ref, ws_ref, bs_ref, wa_ref,
                   gaya_ref, gb_ref, zb_ref, u_scr, v_scr, sg_scr):
    d = D_MODEL
    n_chunks = MIX_IN_SUB // CHUNK
    n_pass = MIX_TILE // MIX_IN_SUB
    rows_of = [slice(s * MIX_IN_SUB, (s + 1) * MIX_IN_SUB) for s in range(n_pass)]

    def front(s):
        h = _rms(x_ref[rows_of[s], :], g_ref[...]).astype(_BF16)
        u_scr[s] = _gelu(_dot(h, win_ref[:, 0:d]))
        return h, _gelu(_dot(h, win_ref[:, d:2 * d]))

    def gate_b(s, h):
        gb_ref[rows_of[s], :] = jax.nn.sigmoid(_dot(h, win_ref[:, 4 * d:5 * d])).astype(_BF16)

    def spatial_gate(s):
        for hd in range(A_HEADS):
            lanes = slice(hd * A_HEAD_DIM, (hd + 1) * A_HEAD_DIM)
            rhs = jnp.concatenate(
                [v_scr[s, c * CHUNK:(c + 1) * CHUNK, lanes] for c in range(n_chunks)], axis=1)
            mixed = _dot(ws_ref[hd], rhs)
            for c in range(n_chunks):
                rows = slice(c * CHUNK, (c + 1) * CHUNK)
                sp = mixed[:, c * A_HEAD_DIM:(c + 1) * A_HEAD_DIM] + bs_ref[:, lanes]
                sg_scr[s, rows, lanes] = (u_scr[s, rows, lanes] * sp).astype(_BF16)

    def branch_a(s, h):
        y_a = _dot(sg_scr[s], wa_ref[...])
        gate_a = jax.nn.sigmoid(_dot(h, win_ref[:, 3 * d:4 * d]))
        gaya_ref[rows_of[s], :] = (gate_a * y_a).astype(_BF16)

    def fourier(s, h):
        zb_ref[rows_of[s], :] = _dot(h, win_ref[:, 2 * d:3 * d]).astype(_BF16)

    h, v = front(0)
    for s in range(n_pass):
        gate_b(s, h)
        v_scr[s] = _rms(v, gv_ref[...]).astype(_BF16)
        if s + 1 < n_pass:
            h_next, v = front(s + 1)
        spatial_gate(s)
        branch_a(s, h)
        fourier(s, h)
        if s + 1 < n_pass:
            h = h_next


def _mix_in(x, g, win, gv, ws, bs, wa):
    t = x.shape[0]
    d = D_MODEL
    tile = pl.BlockSpec((MIX_TILE, d), lambda i: (i, 0))
    return pl.pallas_call(
        _mix_in_kernel,
        grid=(t // MIX_TILE,),
        in_specs=[tile, _resident((1, d)), _resident((d, 5 * d)), _resident((1, d)),
                  _resident((A_HEADS, CHUNK, CHUNK)), _resident((CHUNK, d)), _resident((d, d))],
        out_specs=[tile, tile, tile],
        out_shape=[jax.ShapeDtypeStruct((t, d), _BF16)] * 3,
        scratch_shapes=[pltpu.VMEM((MIX_TILE // MIX_IN_SUB, MIX_IN_SUB, d), dt)
                        for dt in (_F32, _BF16, _BF16)],
        compiler_params=_params(),
        name="mix_in",
    )(x, g, win, gv, ws, bs, wa)


def _fft_a_kernel(zb_ref, dft_ref, f_ref, tc_ref, ts_ref, br_ref, bi_ref, xr_scr, xi_scr):
    n1 = FFT_N1
    gd = B_GROUP_DIM
    dft = dft_ref[...].astype(_BF16)
    f_top, f_bot = f_ref[:n1, :], f_ref[n1:, :]
    z = jnp.swapaxes(zb_ref[...], 0, 1).reshape(FFT_S2_TILE * n1, D_MODEL)
    for grp in range(B_GROUPS):
        cols = slice(grp * gd, (grp + 1) * gd)
        spec = _dot(z[:, cols], dft)
        xr_scr[:, cols] = spec[:, :gd].astype(_BF16)
        xi_scr[:, cols] = spec[:, gd:].astype(_BF16)
    for t in range(FFT_S2_TILE):
        rows = slice(t * n1, (t + 1) * n1)
        tc = tc_ref[:, t:t + 1]
        ts = ts_ref[:, t:t + 1]
        m = jnp.concatenate([tc * f_top + ts * f_bot, tc * f_bot - ts * f_top], axis=0)
        b = _dot(m.astype(_BF16), jnp.concatenate([xr_scr[rows, :], xi_scr[rows, :]], axis=0))
        br_ref[t] = b[:n1].astype(_BF16)
        bi_ref[t] = b[n1:].astype(_BF16)


def _fft_a(zb, dft, fmat, tw_c, tw_s, batch, seq):
    d = D_MODEL
    n1, n2 = FFT_N1, seq // FFT_N1
    m = FFT_S2_TILE
    in_view = (batch, n1, n2, d)
    out_view = (batch, n2, n1, d)
    in_blk = pl.BlockSpec((None, n1, m, d), lambda b, j: (b, 0, j, 0))
    out_blk = pl.BlockSpec((None, m, n1, d), lambda b, j: (b, j, 0, 0))
    tw = pl.BlockSpec((None, n1, m), lambda b, j: (j, 0, 0))
    return pl.pallas_call(
        _fft_a_kernel,
        grid=(batch, n2 // m),
        in_specs=[in_blk, _resident((B_GROUP_DIM, 2 * B_GROUP_DIM)), _resident((2 * n1, 2 * n1)),
                  tw, tw],
        out_specs=[out_blk, out_blk],
        out_shape=[jax.ShapeDtypeStruct(out_view, _BF16)] * 2,
        scratch_shapes=[pltpu.VMEM((m * n1, d), _BF16), pltpu.VMEM((m * n1, d), _BF16)],
        compiler_params=_params(2),
        name="fft_a",
    )(zb.reshape(in_view), dft, fmat, tw_c, tw_s)


def _fft_b_kernel(br_ref, bi_ref, g_ref, wb_ref, yb_ref, *, n2):
    gmat = g_ref[...].astype(_BF16)
    d = br_ref.shape[-1]
    n_slabs = br_ref.shape[1] * n2 // CHUNK
    br = jnp.swapaxes(br_ref[...], 0, 1).reshape(n_slabs, CHUNK, d)
    bi = jnp.swapaxes(bi_ref[...], 0, 1).reshape(n_slabs, CHUNK, d)
    outs = []
    for first in range(0, n_slabs, FFT_B_GROUP):
        fourier = [_dot(gmat, jnp.concatenate([br[t], bi[t]], axis=0)).astype(_BF16)
                   for t in range(first, first + FFT_B_GROUP)]
        y_b = _dot(jnp.concatenate(fourier, axis=0), wb_ref[...]).astype(_BF16)
        outs.append(y_b.reshape(FFT_B_GROUP * CHUNK // n2, n2, d))
    yb_ref[...] = jnp.swapaxes(jnp.concatenate(outs, axis=0), 0, 1)


def _fft_b(br, bi, gmat, wb, batch, seq):
    d = D_MODEL
    n1, n2 = FFT_N1, seq // FFT_N1
    k1s = FFT_B_ROWS // n2
    blk = pl.BlockSpec((None, n2, k1s, d), lambda b, j: (b, 0, j, 0))
    y_b = pl.pallas_call(
        functools.partial(_fft_b_kernel, n2=n2),
        grid=(batch, n1 // k1s),
        in_specs=[blk, blk, _resident((CHUNK, 2 * CHUNK)), _resident((d, d))],
        out_specs=blk,
        out_shape=jax.ShapeDtypeStruct((batch, n2, n1, d), _BF16),
        compiler_params=_params(2),
        name="fft_b",
    )(br, bi, gmat, wb)
    return y_b.reshape(batch * seq, d)


def _mix_out_ffn_kernel(x_ref, gaya_ref, gb_ref, yb_ref, wo_ref, g_ref, wg_ref, wu_ref,
                        wd_ref, fg_ref, o_ref, a_ref, *, final):
    n_pass = MIX_TILE // MIX_SUB
    rows_of = [slice(s * MIX_SUB, (s + 1) * MIX_SUB) for s in range(n_pass)]
    half_cols = (D_FF // FF_COLS // 2) * FF_COLS

    def front(s):
        rows = rows_of[s]
        gated = (gaya_ref[rows, :].astype(_F32)
                 + gb_ref[rows, :].astype(_F32) * yb_ref[rows, :].astype(_F32))
        x = x_ref[rows, :] + _dot(gated.astype(_BF16), wo_ref[...])
        return x, _rms(x, g_ref[...]).astype(_BF16)

    x, h = front(0)
    for s in range(n_pass):
        _swiglu_hidden(h, wg_ref, wu_ref, a_ref.at[s], 0, half_cols)
        if s + 1 < n_pass:
            x_next, h_next = front(s + 1)
        _swiglu_hidden(h, wg_ref, wu_ref, a_ref.at[s], half_cols, D_FF)
        out = x + 0.5 * _dot(a_ref[s], wd_ref[...])
        if final:
            out = _rms(out, fg_ref[...])
        o_ref[rows_of[s], :] = out
        if s + 1 < n_pass:
            x, h = x_next, h_next


def _mix_out_ffn(x, gaya, gb, y_b, wo, g, wg, wu, wd, fg, final):
    t = x.shape[0]
    d = D_MODEL
    tile = pl.BlockSpec((MIX_TILE, d), lambda i: (i, 0))
    return pl.pallas_call(
        functools.partial(_mix_out_ffn_kernel, final=final),
        grid=(t // MIX_TILE,),
        in_specs=[tile, tile, tile, tile, _resident((d, d))] + _ffn_specs() + [_resident((1, d))],
        out_specs=tile,
        out_shape=jax.ShapeDtypeStruct((t, d), _F32),
        scratch_shapes=[pltpu.VMEM((MIX_TILE // MIX_SUB, MIX_SUB, D_FF), _BF16)],
        compiler_params=_params(),
        name="mix_out_ffn",
    )(x, gaya, gb, y_b, wo, g, wg, wu, wd, fg)


def _cos_sin(num, den):
    ang = 2.0 * np.pi * (np.asarray(num, dtype=np.int64) % den).astype(np.float64) / den
    return np.cos(ang), np.sin(ang)


def _channel_dft():
    n = B_GROUP_DIM
    c, s = _cos_sin(np.outer(np.arange(n), np.arange(n)), n)
    return jnp.asarray(np.concatenate([c, -s], axis=1) / np.sqrt(n), dtype=_F32)


def _position_dft(seq):
    n1, n2 = FFT_N1, seq // FFT_N1
    c1, s1 = _cos_sin(np.outer(np.arange(n1), np.arange(n1)), n1)
    fmat = np.block([[c1, s1], [-s1, c1]]) / np.sqrt(seq)
    tc, ts = _cos_sin(np.outer(np.arange(n1), np.arange(n2)), seq)
    m = FFT_S2_TILE
    tc = tc.reshape(n1, n2 // m, m).transpose(1, 0, 2)
    ts = ts.reshape(n1, n2 // m, m).transpose(1, 0, 2)
    c2, s2 = _cos_sin(np.outer(np.arange(n2), np.arange(n2)), n2)
    eye = np.eye(CHUNK // n2)
    gmat = np.concatenate([np.kron(eye, c2), np.kron(eye, s2)], axis=1)
    return tuple(jnp.asarray(t, dtype=_F32) for t in (fmat, tc, ts, gmat))


def kernel(x_prompt, x_sample, ffn1_norm, ffn1_w_gate, ffn1_w_up, ffn1_w_down, mix_norm, w_in,
           sgu_norm, sgu_w, sgu_b, w_branch_a, w_branch_b, w_out, ffn2_norm, ffn2_w_gate,
           ffn2_w_up, ffn2_w_down, final_norm):
    depth = w_in.shape[0]
    d = D_MODEL
    bf = lambda w: w.astype(_BF16)
    row = lambda v: v.reshape(1, d).astype(_F32)
    dft = _channel_dft()
    bias = jnp.repeat(jnp.swapaxes(sgu_b, 1, 2), A_HEAD_DIM, axis=2).astype(_F32)
    weights = dict(
        wg1=bf(ffn1_w_gate), wu1=bf(ffn1_w_up), wd1=bf(ffn1_w_down), win=bf(w_in), ws=bf(sgu_w),
        wa=bf(w_branch_a), wb=bf(w_branch_b), wo=bf(w_out), wg2=bf(ffn2_w_gate),
        wu2=bf(ffn2_w_up), wd2=bf(ffn2_w_down))
    fg = row(final_norm)

    def trunk(x3):
        batch, seq, _ = x3.shape
        fmat, tw_c, tw_s, gmat = _position_dft(seq)
        x = x3.reshape(batch * seq, d)
        for l in range(depth):
            w = {k: v[l] for k, v in weights.items()}
            x = _ffn(x, row(ffn1_norm[l]), w["wg1"], w["wu1"], w["wd1"])
            gaya, gb, zb = _mix_in(x, row(mix_norm[l]), w["win"], row(sgu_norm[l]), w["ws"],
                                   bias[l], w["wa"])
            br, bi = _fft_a(zb, dft, fmat, tw_c, tw_s, batch, seq)
            y_b = _fft_b(br, bi, gmat, w["wb"], batch, seq)
            x = _mix_out_ffn(x, gaya, gb, y_b, w["wo"], row(ffn2_norm[l]), w["wg2"],
                             w["wu2"], w["wd2"], fg, final=(l == depth - 1))
        return x.reshape(batch, seq, d)

    return (trunk(x_prompt), trunk(x_sample))
```

```python
import functools

import numpy as np
import jax
import jax.numpy as jnp
from jax import lax
from jax.experimental import pallas as pl
from jax.experimental.pallas import tpu as pltpu

D_MODEL = 1024
CHUNK = 128
A_HEADS = 8
A_HEAD_DIM = D_MODEL // A_HEADS
B_GROUPS = 4
B_GROUP_DIM = D_MODEL // B_GROUPS
D_FF = 2816
EPS = 1e-6

FFN_TILE = 1024
FFN_SUB = 256
MIX_TILE = 1024
MIX_IN_SUB = 256
MIX_SUB = 512
FF_COLS = 256
FFT_N1 = 128
FFT_S2_TILE = 16
FFT_B_ROWS = 2048
FFT_B_GROUP = 4
V7X_VMEM_BYTES = 64 * 1024 * 1024
VMEM_LIMIT_BYTES = V7X_VMEM_BYTES * 7 // 8

_BF16 = jnp.bfloat16
_F32 = jnp.float32


def _dot(a, b):
    return jnp.dot(a, b, preferred_element_type=_F32)


def _rms(x, g):
    return x * lax.rsqrt(jnp.mean(x * x, axis=-1, keepdims=True) + EPS) * g


def _gelu(x):
    return 0.5 * x * (1.0 + lax.erf(x * np.float32(np.sqrt(0.5))))


def _resident(shape):
    return pl.BlockSpec(shape, lambda *_: (0,) * len(shape), pipeline_mode=pl.Buffered(1))


def _params(n_grid_dims=1):
    return pltpu.CompilerParams(dimension_semantics=("arbitrary",) * n_grid_dims,
                                vmem_limit_bytes=VMEM_LIMIT_BYTES)


def _swiglu_hidden(h, wg_ref, wu_ref, a_ref, first_col, last_col):
    for c in range(first_col, last_col, FF_COLS):
        gate = _dot(h, wg_ref[:, c:c + FF_COLS])
        up = _dot(h, wu_ref[:, c:c + FF_COLS])
        a_ref[:, c:c + FF_COLS] = (jax.nn.silu(gate) * up).astype(_BF16)


def _ffn_kernel(x_ref, g_ref, wg_ref, wu_ref, wd_ref, o_ref, a_ref):
    n_pass = FFN_TILE // FFN_SUB
    rows_of = [slice(s * FFN_SUB, (s + 1) * FFN_SUB) for s in range(n_pass)]
    half_cols = (D_FF // FF_COLS // 2) * FF_COLS

    def front(s):
        x = x_ref[rows_of[s], :]
        return x, _rms(x, g_ref[...]).astype(_BF16)

    x, h = front(0)
    for s in range(n_pass):
        _swiglu_hidden(h, wg_ref, wu_ref, a_ref.at[s], 0, half_cols)
        if s + 1 < n_pass:
            x_next, h_next = front(s + 1)
        _swiglu_hidden(h, wg_ref, wu_ref, a_ref.at[s], half_cols, D_FF)
        o_ref[rows_of[s], :] = x + 0.5 * _dot(a_ref[s], wd_ref[...])
        if s + 1 < n_pass:
            x, h = x_next, h_next


def _ffn_specs():
    return [_resident((1, D_MODEL)), _resident((D_MODEL, D_FF)), _resident((D_MODEL, D_FF)),
            _resident((D_FF, D_MODEL))]


def _ffn(x, g, wg, wu, wd):
    t = x.shape[0]
    tile = pl.BlockSpec((FFN_TILE, D_MODEL), lambda i: (i, 0))
    return pl.pallas_call(
        _ffn_kernel,
        grid=(t // FFN_TILE,),
        in_specs=[tile] + _ffn_specs(),
        out_specs=tile,
        out_shape=jax.ShapeDtypeStruct((t, D_MODEL), _F32),
        scratch_shapes=[pltpu.VMEM((FFN_TILE // FFN_SUB, FFN_SUB, D_FF), _BF16)],
        compiler_params=_params(),
        name="ffn",
    )(x, g, wg, wu, wd)


def _mix_in_kernel(x_ref, g_ref, win_ref, gv_ref, ws_ref, bs_ref, wa_ref,
                   gaya_ref, gb_ref, zb_ref, u_scr, v_scr, sg_scr):
    d = D_MODEL
    n_chunks = MIX_IN_SUB // CHUNK
    n_pass = MIX_TILE // MIX_IN_SUB
    rows_of = [slice(s * MIX_IN_SUB, (s + 1) * MIX_IN_SUB) for s in range(n_pass)]

    def front(s):
        h = _rms(x_ref[rows_of[s], :], g_ref[...]).astype(_BF16)
        u_scr[s] = _gelu(_dot(h, win_ref[:, 0:d]))
        return h, _gelu(_dot(h, win_ref[:, d:2 * d]))

    def gate_b(s, h):
        gb_ref[rows_of[s], :] = jax.nn.sigmoid(_dot(h, win_ref[:, 4 * d:5 * d])).astype(_BF16)

    def spatial_gate(s):
        for hd in range(A_HEADS):
            lanes = slice(hd * A_HEAD_DIM, (hd + 1) * A_HEAD_DIM)
            rhs = jnp.concatenate(
                [v_scr[s, c * CHUNK:(c + 1) * CHUNK, lanes] for c in range(n_chunks)], axis=1)
            mixed = _dot(ws_ref[hd], rhs)
            for c in range(n_chunks):
                rows = slice(c * CHUNK, (c + 1) * CHUNK)
                sp = mixed[:, c * A_HEAD_DIM:(c + 1) * A_HEAD_DIM] + bs_ref[:, lanes]
                sg_scr[s, rows, lanes] = (u_scr[s, rows, lanes] * sp).astype(_BF16)

    def branch_a(s, h):
        y_a = _dot(sg_scr[s], wa_ref[...])
        gate_a = jax.nn.sigmoid(_dot(h, win_ref[:, 3 * d:4 * d]))
        gaya_ref[rows_of[s], :] = (gate_a * y_a).astype(_BF16)

    def fourier(s, h):
        zb_ref[rows_of[s], :] = _dot(h, win_ref[:, 2 * d:3 * d]).astype(_BF16)

    h, v = front(0)
    for s in range(n_pass):
        gate_b(s, h)
        v_scr[s] = _rms(v, gv_ref[...]).astype(_BF16)
        if s + 1 < n_pass:
            h_next, v = front(s + 1)
        spatial_gate(s)
        branch_a(s, h)
        fourier(s, h)
        if s + 1 < n_pass:
            h = h_next


def _mix_in(x, g, win, gv, ws, bs, wa):
    t = x.shape[0]
    d = D_MODEL
    tile = pl.BlockSpec((MIX_TILE, d), lambda i: (i, 0))
    return pl.pallas_call(
        _mix_in_kernel,
        grid=(t // MIX_TILE,),
        in_specs=[tile, _resident((1, d)), _resident((d, 5 * d)), _resident((1, d)),
                  _resident((A_HEADS, CHUNK, CHUNK)), _resident((CHUNK, d)), _resident((d, d))],
        out_specs=[tile, tile, tile],
        out_shape=[jax.ShapeDtypeStruct((t, d), _BF16)] * 3,
        scratch_shapes=[pltpu.VMEM((MIX_TILE // MIX_IN_SUB, MIX_IN_SUB, d), dt)
                        for dt in (_F32, _BF16, _BF16)],
        compiler_params=_params(),
        name="mix_in",
    )(x, g, win, gv, ws, bs, wa)


def _fft_a_kernel(zb_ref, dft_ref, f_ref, tc_ref, ts_ref, br_ref, bi_ref, xr_scr, xi_scr):
    n1 = FFT_N1
    gd = B_GROUP_DIM
    dft = dft_ref[...].astype(_BF16)
    f_top, f_bot = f_ref[:n1, :], f_ref[n1:, :]
    z = jnp.swapaxes(zb_ref[...], 0, 1).reshape(FFT_S2_TILE * n1, D_MODEL)
    for grp in range(B_GROUPS):
        cols = slice(grp * gd, (grp + 1) * gd)
        spec = _dot(z[:, cols], dft)
        xr_scr[:, cols] = spec[:, :gd].astype(_BF16)
        xi_scr[:, cols] = spec[:, gd:].astype(_BF16)
    for t in range(FFT_S2_TILE):
        rows = slice(t * n1, (t + 1) * n1)
        tc = tc_ref[:, t:t + 1]
        ts = ts_ref[:, t:t + 1]
        m = jnp.concatenate([tc * f_top + ts * f_bot, tc * f_bot - ts * f_top], axis=0)
        b = _dot(m.astype(_BF16), jnp.concatenate([xr_scr[rows, :], xi_scr[rows, :]], axis=0))
        br_ref[t] = b[:n1].astype(_BF16)
        bi_ref[t] = b[n1:].astype(_BF16)


def _fft_a(zb, dft, fmat, tw_c, tw_s, batch, seq):
    d = D_MODEL
    n1, n2 = FFT_N1, seq // FFT_N1
    m = FFT_S2_TILE
    in_view = (batch, n1, n2, d)
    out_view = (batch, n2, n1, d)
    in_blk = pl.BlockSpec((None, n1, m, d), lambda b, j: (b, 0, j, 0))
    out_blk = pl.BlockSpec((None, m, n1, d), lambda b, j: (b, j, 0, 0))
    tw = pl.BlockSpec((None, n1, m), lambda b, j: (j, 0, 0))
    return pl.pallas_call(
        _fft_a_kernel,
        grid=(batch, n2 // m),
        in_specs=[in_blk, _resident((B_GROUP_DIM, 2 * B_GROUP_DIM)), _resident((2 * n1, 2 * n1)),
                  tw, tw],
        out_specs=[out_blk, out_blk],
        out_shape=[jax.ShapeDtypeStruct(out_view, _BF16)] * 2,
        scratch_shapes=[pltpu.VMEM((m * n1, d), _BF16), pltpu.VMEM((m * n1, d), _BF16)],
        compiler_params=_params(2),
        name="fft_a",
    )(zb.reshape(in_view), dft, fmat, tw_c, tw_s)


def _fft_b_kernel(br_ref, bi_ref, g_ref, wb_ref, yb_ref, *, n2):
    gmat = g_ref[...].astype(_BF16)
    d = br_ref.shape[-1]
    n_slabs = br_ref.shape[1] * n2 // CHUNK
    br = jnp.swapaxes(br_ref[...], 0, 1).reshape(n_slabs, CHUNK, d)
    bi = jnp.swapaxes(bi_ref[...], 0, 1).reshape(n_slabs, CHUNK, d)
    outs = []
    for first in range(0, n_slabs, FFT_B_GROUP):
        fourier = [_dot(gmat, jnp.concatenate([br[t], bi[t]], axis=0)).astype(_BF16)
                   for t in range(first, first + FFT_B_GROUP)]
        y_b = _dot(jnp.concatenate(fourier, axis=0), wb_ref[...]).astype(_BF16)
        outs.append(y_b.reshape(FFT_B_GROUP * CHUNK // n2, n2, d))
    yb_ref[...] = jnp.swapaxes(jnp.concatenate(outs, axis=0), 0, 1)


def _fft_b(br, bi, gmat, wb, batch, seq):
    d = D_MODEL
    n1, n2 = FFT_N1, seq // FFT_N1
    k1s = FFT_B_ROWS // n2
    blk = pl.BlockSpec((None, n2, k1s, d), lambda b, j: (b, 0, j, 0))
    y_b = pl.pallas_call(
        functools.partial(_fft_b_kernel, n2=n2),
        grid=(batch, n1 // k1s),
        in_specs=[blk, blk, _resident((CHUNK, 2 * CHUNK)), _resident((d, d))],
        out_specs=blk,
        out_shape=jax.ShapeDtypeStruct((batch, n2, n1, d), _BF16),
        compiler_params=_params(2),
        name="fft_b",
    )(br, bi, gmat, wb)
    return y_b.reshape(batch * seq, d)


def _mix_out_ffn_kernel(x_ref, gaya_ref, gb_ref, yb_ref, wo_ref, g_ref, wg_ref, wu_ref,
                        wd_ref, fg_ref, o_ref, a_ref, *, final):
    n_pass = MIX_TILE // MIX_SUB
    rows_of = [slice(s * MIX_SUB, (s + 1) * MIX_SUB) for s in range(n_pass)]
    half_cols = (D_FF // FF_COLS // 2) * FF_COLS

    def front(s):
        rows = rows_of[s]
        gated = (gaya_ref[rows, :].astype(_F32)
                 + gb_ref[rows, :].astype(_F32) * yb_ref[rows, :].astype(_F32))
        x = x_ref[rows, :] + _dot(gated.astype(_BF16), wo_ref[...])
        return x, _rms(x, g_ref[...]).astype(_BF16)

    x, h = front(0)
    for s in range(n_pass):
        _swiglu_hidden(h, wg_ref, wu_ref, a_ref.at[s], 0, half_cols)
        if s + 1 < n_pass:
            x_next, h_next = front(s + 1)
        _swiglu_hidden(h, wg_ref, wu_ref, a_ref.at[s], half_cols, D_FF)
        out = x + 0.5 * _dot(a_ref[s], wd_ref[...])
        if final:
            out = _rms(out, fg_ref[...])
        o_ref[rows_of[s], :] = out
        if s + 1 < n_pass:
            x, h = x_next, h_next


def _mix_out_ffn(x, gaya, gb, y_b, wo, g, wg, wu, wd, fg, final):
    t = x.shape[0]
    d = D_MODEL
    tile = pl.BlockSpec((MIX_TILE, d), lambda i: (i, 0))
    return pl.pallas_call(
        functools.partial(_mix_out_ffn_kernel, final=final),
        grid=(t // MIX_TILE,),
        in_specs=[tile, tile, tile, tile, _resident((d, d))] + _ffn_specs() + [_resident((1, d))],
        out_specs=tile,
        out_shape=jax.ShapeDtypeStruct((t, d), _F32),
        scratch_shapes=[pltpu.VMEM((MIX_TILE // MIX_SUB, MIX_SUB, D_FF), _BF16)],
        compiler_params=_params(),
        name="mix_out_ffn",
    )(x, gaya, gb, y_b, wo, g, wg, wu, wd, fg)


def _cos_sin(num, den):
    ang = 2.0 * np.pi * (np.asarray(num, dtype=np.int64) % den).astype(np.float64) / den
    return np.cos(ang), np.sin(ang)


def _channel_dft():
    n = B_GROUP_DIM
    c, s = _cos_sin(np.outer(np.arange(n), np.arange(n)), n)
    return jnp.asarray(np.concatenate([c, -s], axis=1) / np.sqrt(n), dtype=_F32)


def _position_dft(seq):
    n1, n2 = FFT_N1, seq // FFT_N1
    c1, s1 = _cos_sin(np.outer(np.arange(n1), np.arange(n1)), n1)
    fmat = np.block([[c1, s1], [-s1, c1]]) / np.sqrt(seq)
    tc, ts = _cos_sin(np.outer(np.arange(n1), np.arange(n2)), seq)
    m = FFT_S2_TILE
    tc = tc.reshape(n1, n2 // m, m).transpose(1, 0, 2)
    ts = ts.reshape(n1, n2 // m, m).transpose(1, 0, 2)
    c2, s2 = _cos_sin(np.outer(np.arange(n2), np.arange(n2)), n2)
    eye = np.eye(CHUNK // n2)
    gmat = np.concatenate([np.kron(eye, c2), np.kron(eye, s2)], axis=1)
    return tuple(jnp.asarray(t, dtype=_F32) for t in (fmat, tc, ts, gmat))


def kernel(x_prompt, x_sample, ffn1_norm, ffn1_w_gate, ffn1_w_up, ffn1_w_down, mix_norm, w_in,
           sgu_norm, sgu_w, sgu_b, w_branch_a, w_branch_b, w_out, ffn2_norm, ffn2_w_gate,
           ffn2_w_up, ffn2_w_down, final_norm):
    depth = w_in.shape[0]
    d = D_MODEL
    bf = lambda w: w.astype(_BF16)
    row = lambda v: v.reshape(1, d).astype(_F32)
    dft = _channel_dft()
    bias = jnp.repeat(jnp.swapaxes(sgu_b, 1, 2), A_HEAD_DIM, axis=2).astype(_F32)
    weights = dict(
        wg1=bf(ffn1_w_gate), wu1=bf(ffn1_w_up), wd1=bf(ffn1_w_down), win=bf(w_in), ws=bf(sgu_w),
        wa=bf(w_branch_a), wb=bf(w_branch_b), wo=bf(w_out), wg2=bf(ffn2_w_gate),
        wu2=bf(ffn2_w_up), wd2=bf(ffn2_w_down))
    fg = row(final_norm)

    def trunk(x3):
        batch, seq, _ = x3.shape
        fmat, tw_c, tw_s, gmat = _position_dft(seq)
        x = x3.reshape(batch * seq, d)
        for l in range(depth):
            w = {k: v[l] for k, v in weights.items()}
            x = _ffn(x, row(ffn1_norm[l]), w["wg1"], w["wu1"], w["wd1"])
            gaya, gb, zb = _mix_in(x, row(mix_norm[l]), w["win"], row(sgu_norm[l]), w["ws"],
                                   bias[l], w["wa"])
            br, bi = _fft_a(zb, dft, fmat, tw_c, tw_s, batch, seq)
            y_b = _fft_b(br, bi, gmat, w["wb"], batch, seq)
            x = _mix_out_ffn(x, gaya, gb, y_b, w["wo"], row(ffn2_norm[l]), w["wg2"],
                             w["wu2"], w["wd2"], fg, final=(l == depth - 1))
        return x.reshape(batch, seq, d)

    return (trunk(x_prompt), trunk(x_sample))
```
